```python
import jax, jax.numpy as jnp
from jax import lax
import numpy as np

D_MODEL = 1024
BATCH = 4
SEQ = 4096
DEPTH = 4

N_MIXERS = 2
N_RET_LAYERS = (DEPTH + 1) // 2
N_GDN_LAYERS = DEPTH // 2

RET_HEADS = 4
RET_HEAD_QK = D_MODEL // RET_HEADS
RET_HEAD_V = 2 * RET_HEAD_QK
RET_QK = RET_HEADS * RET_HEAD_QK
RET_V = RET_HEADS * RET_HEAD_V
RET_IN = 2 * RET_QK + 2 * RET_V
RET_CHUNK = 128
ROPE_THETA = 10000.0

GDN_HEAD_K = 128
GDN_HEAD_V = 128
GDN_K_HEADS = D_MODEL // GDN_HEAD_K
GDN_V_HEADS = 2 * GDN_K_HEADS
GDN_QK = GDN_K_HEADS * GDN_HEAD_K
GDN_V = GDN_V_HEADS * GDN_HEAD_V
GDN_CONV_DIM = 2 * GDN_QK + GDN_V
GDN_IN = GDN_CONV_DIM + GDN_V + 2 * GDN_V_HEADS
GDN_CONV_K = 4
GDN_CHUNK = 64

FFN_HIDDEN = -(-8 * D_MODEL // (3 * 256)) * 256
NORM_EPS = 1e-6

kernel_name = 'hybrid_retention_gated_deltanet_trunk'


def rmsnorm(x, gain):
    xf = x.astype(jnp.float32)
    y = xf * lax.rsqrt(jnp.mean(xf * xf, axis=-1, keepdims=True) + NORM_EPS)
    return (y * gain.astype(jnp.float32)).astype(x.dtype)


def rope(t, positions):
    half = t.shape[-1] // 2
    inv_freq = ROPE_THETA ** (-jnp.arange(half, dtype=jnp.float32) / half)
    ang = positions.astype(jnp.float32)[..., None] * inv_freq
    cos = jnp.cos(ang)[:, :, None, :]
    sin = jnp.sin(ang)[:, :, None, :]
    t1, t2 = t[..., :half], t[..., half:]
    return jnp.concatenate([t1 * cos - t2 * sin, t2 * cos + t1 * sin], axis=-1)


def causal_depthwise_conv(x, w):
    c = x.shape[-1]
    return lax.conv_general_dilated(
        x, w.astype(x.dtype)[:, None, :], window_strides=(1,),
        padding=[(GDN_CONV_K - 1, 0)], dimension_numbers=('NWC', 'WIO', 'NWC'),
        feature_group_count=c)


def to_chunks(t, c):
    b, s, h, d = t.shape
    return t.reshape(b, s // c, c, h, d).transpose(0, 3, 1, 2, 4)


def retention_chunked(q, k, v):
    b, s, h, dk = q.shape
    dv = v.shape[-1]
    c = RET_CHUNK
    log_gamma = jnp.log1p(-jnp.exp2(-5.0 - jnp.arange(h, dtype=jnp.float32)))
    qc, kc, vc = to_chunks(q, c), to_chunks(k, c), to_chunks(v, c)
    pos = jnp.arange(c, dtype=jnp.float32)
    causal = pos[:, None] >= pos[None, :]
    diff = jnp.where(causal, pos[:, None] - pos[None, :], 0.0)
    decay = jnp.where(causal, jnp.exp(log_gamma[:, None, None] * diff), 0.0)
    scores = jnp.einsum('bhncd,bhnmd->bhncm', qc, kc) * decay[None, :, None]
    inner = jnp.einsum('bhncm,bhnmv->bhncv', scores, vc)
    q_dec = qc * jnp.exp(log_gamma[:, None] * (pos + 1.0))[None, :, None, :, None]
    k_dec = kc * jnp.exp(log_gamma[:, None] * (c - 1.0 - pos))[None, :, None, :, None]
    chunk_decay = jnp.exp(log_gamma * c)[None, :, None, None]

    def step(state, xs):
        qd, kd, vi = xs
        o = jnp.einsum('bhcd,bhdv->bhcv', qd, state)
        state = state * chunk_decay + jnp.einsum('bhcd,bhcv->bhdv', kd, vi)
        return state, o

    xs = (jnp.moveaxis(q_dec, 2, 0), jnp.moveaxis(k_dec, 2, 0), jnp.moveaxis(vc, 2, 0))
    _, cross = lax.scan(step, jnp.zeros((b, h, dk, dv), jnp.float32), xs)
    out = inner + jnp.moveaxis(cross, 0, 2)
    return out.transpose(0, 2, 3, 1, 4).reshape(b, s, h, dv)


def gated_delta_chunked(q, k, v, g, beta):
    b, s, h, dk = q.shape
    dv = v.shape[-1]
    c = GDN_CHUNK
    n = s // c
    qc, kc, vc = to_chunks(q, c), to_chunks(k, c), to_chunks(v, c)
    gc = g.reshape(b, n, c, h).transpose(0, 3, 1, 2)
    bc = beta.reshape(b, n, c, h).transpose(0, 3, 1, 2)
    gcum = jnp.cumsum(gc, axis=-1)
    idx = jnp.arange(c)
    causal = idx[:, None] >= idx[None, :]
    strict = idx[:, None] > idx[None, :]
    gdiff = jnp.where(causal, gcum[..., :, None] - gcum[..., None, :], 0.0)
    decay_mat = jnp.where(causal, jnp.exp(gdiff), 0.0)
    k_beta = kc * bc[..., None]
    a_mat = jnp.where(strict, jnp.einsum('bhncd,bhnmd->bhncm', k_beta, kc) * decay_mat, 0.0)
    lhs = a_mat + jnp.eye(c, dtype=jnp.float32)
    rhs = jnp.concatenate([vc * bc[..., None], k_beta * jnp.exp(gcum)[..., None]], axis=-1)
    sol = lax.linalg.triangular_solve(lhs, rhs, left_side=True, lower=True, unit_diagonal=True)
    u = sol[..., :dv]
    w = sol[..., dv:]
    attn = jnp.where(causal, jnp.einsum('bhncd,bhnmd->bhncm', qc, kc) * decay_mat, 0.0)
    q_dec = qc * jnp.exp(gcum)[..., None]
    g_last = gcum[..., -1]
    k_dec = kc * jnp.exp(g_last[..., None] - gcum)[..., None]

    def step(state, xs):
        ui, wi, qi, ki, ai, gl = xs
        v_new = ui - jnp.einsum('bhcd,bhdv->bhcv', wi, state)
        o = jnp.einsum('bhcd,bhdv->bhcv', qi, state) + jnp.einsum('bhcm,bhmv->bhcv', ai, v_new)
        state = state * jnp.exp(gl)[..., None, None] + jnp.einsum('bhcd,bhcv->bhdv', ki, v_new)
        return state, o

    xs = tuple(jnp.moveaxis(t, 2, 0) for t in (u, w, q_dec, k_dec, attn, g_last))
    _, out = lax.scan(step, jnp.zeros((b, h, dk, dv), jnp.float32), xs)
    out = jnp.moveaxis(out, 0, 2)
    return out.transpose(0, 2, 3, 1, 4).reshape(b, s, h, dv)


def retention_mixer(hdn, positions, w_in, gn_gain, w_out):
    b, s, _ = hdn.shape
    proj = hdn @ w_in
    q, k, v, gate = jnp.split(proj, [RET_QK, 2 * RET_QK, 2 * RET_QK + RET_V], axis=-1)
    q = rope(q.reshape(b, s, RET_HEADS, RET_HEAD_QK).astype(jnp.float32), positions)
    k = rope(k.reshape(b, s, RET_HEADS, RET_HEAD_QK).astype(jnp.float32), positions) * (RET_HEAD_QK ** -0.5)
    v = v.reshape(b, s, RET_HEADS, RET_HEAD_V).astype(jnp.float32)
    o = retention_chunked(q, k, v)
    mu = jnp.mean(o, axis=-1, keepdims=True)
    var = jnp.mean(jnp.square(o - mu), axis=-1, keepdims=True)
    o = ((o - mu) * lax.rsqrt(var + NORM_EPS)).reshape(b, s, RET_V) * gn_gain.astype(jnp.float32)
    y = jax.nn.silu(gate.astype(jnp.float32)) * o
    return y.astype(hdn.dtype) @ w_out


def l2norm(t):
    return t * lax.rsqrt(jnp.sum(t * t, axis=-1, keepdims=True) + NORM_EPS)


def gdn_mixer(hdn, w_in, conv_w, a_log, dt_bias, norm_gain, w_out):
    b, s, _ = hdn.shape
    proj = hdn @ w_in
    qkv, z, beta_logit, a = jnp.split(
        proj, [GDN_CONV_DIM, GDN_CONV_DIM + GDN_V, GDN_CONV_DIM + GDN_V + GDN_V_HEADS], axis=-1)
    qkv = jax.nn.silu(causal_depthwise_conv(qkv, conv_w))
    q, k, v = jnp.split(qkv, [GDN_QK, 2 * GDN_QK], axis=-1)
    rep = GDN_V_HEADS // GDN_K_HEADS
    q = jnp.repeat(l2norm(q.reshape(b, s, GDN_K_HEADS, GDN_HEAD_K).astype(jnp.float32)), rep, axis=2)
    q = q * (GDN_HEAD_K ** -0.5)
    k = jnp.repeat(l2norm(k.reshape(b, s, GDN_K_HEADS, GDN_HEAD_K).astype(jnp.float32)), rep, axis=2)
    v = v.reshape(b, s, GDN_V_HEADS, GDN_HEAD_V).astype(jnp.float32)
    beta = jax.nn.sigmoid(beta_logit.astype(jnp.float32))
    g = -jnp.exp(a_log.astype(jnp.float32)) * jax.nn.softplus(
        a.astype(jnp.float32) + dt_bias.astype(jnp.float32))
    o = gated_delta_chunked(q, k, v, g, beta)
    o = o * lax.rsqrt(jnp.mean(o * o, axis=-1, keepdims=True) + NORM_EPS) * norm_gain.astype(jnp.float32)
    o = o.reshape(b, s, GDN_V) * jax.nn.silu(z.astype(jnp.float32))
    return o.astype(hdn.dtype) @ w_out


def swiglu(hdn, w_in, w_out):
    gate, up = jnp.split(hdn @ w_in, 2, axis=-1)
    return (jax.nn.silu(gate) * up) @ w_out


def setup_inputs(seed: int = 0) -> dict:
    key = jax.random.key(seed)
    ks = jax.random.split(key, 20)
    f32 = jnp.float32
    nrm = lambda k, shape, scale: jax.random.normal(k, shape, f32) * scale
    out_scale = 0.5
    x = jax.random.normal(ks[0], (BATCH, SEQ, D_MODEL), f32)
    positions = jnp.broadcast_to(jnp.arange(SEQ, dtype=jnp.int32), (BATCH, SEQ))
    norm_mix = 1.0 + nrm(ks[1], (DEPTH, D_MODEL), 0.02)
    norm_ffn = 1.0 + nrm(ks[2], (DEPTH, D_MODEL), 0.02)
    norm_final = 1.0 + nrm(ks[3], (D_MODEL,), 0.02)
    ret_w_in = nrm(ks[4], (N_RET_LAYERS, D_MODEL, RET_IN), D_MODEL ** -0.5)
    ret_gn_gain = 1.0 + nrm(ks[5], (N_RET_LAYERS, RET_V), 0.02)
    ret_w_out = nrm(ks[6], (N_RET_LAYERS, RET_V, D_MODEL), out_scale * RET_V ** -0.5)
    gdn_w_in = nrm(ks[7], (N_GDN_LAYERS, D_MODEL, GDN_IN), D_MODEL ** -0.5)
    gdn_conv = nrm(ks[8], (N_GDN_LAYERS, GDN_CONV_K, GDN_CONV_DIM), GDN_CONV_K ** -0.5)
    gdn_a_log = jnp.log(jax.random.uniform(ks[9], (N_GDN_LAYERS, GDN_V_HEADS), f32, 1.0, 16.0))
    dt = jnp.exp(jax.random.uniform(ks[10], (N_GDN_LAYERS, GDN_V_HEADS), f32,
                                    float(np.log(1e-3)), float(np.log(1e-1))))
    gdn_dt_bias = dt + jnp.log(-jnp.expm1(-dt))
    gdn_norm_gain = 1.0 + nrm(ks[11], (N_GDN_LAYERS, GDN_HEAD_V), 0.02)
    gdn_w_out = nrm(ks[12], (N_GDN_LAYERS, GDN_V, D_MODEL), out_scale * GDN_V ** -0.5)
    ffn_w_in = nrm(ks[13], (DEPTH, D_MODEL, 2 * FFN_HIDDEN), D_MODEL ** -0.5)
    ffn_w_out = nrm(ks[14], (DEPTH, FFN_HIDDEN, D_MODEL), out_scale * FFN_HIDDEN ** -0.5)
    return {'x': x, 'positions': positions, 'norm_mix': norm_mix, 'norm_ffn': norm_ffn,
            'norm_final': norm_final, 'ret_w_in': ret_w_in, 'ret_gn_gain': ret_gn_gain,
            'ret_w_out': ret_w_out, 'gdn_w_in': gdn_w_in, 'gdn_conv': gdn_conv,
            'gdn_a_log': gdn_a_log, 'gdn_dt_bias': gdn_dt_bias, 'gdn_norm_gain': gdn_norm_gain,
            'gdn_w_out': gdn_w_out, 'ffn_w_in': ffn_w_in, 'ffn_w_out': ffn_w_out}


def reference(x, positions, norm_mix, norm_ffn, norm_final, ret_w_in, ret_gn_gain, ret_w_out,
              gdn_w_in, gdn_conv, gdn_a_log, gdn_dt_bias, gdn_norm_gain, gdn_w_out,
              ffn_w_in, ffn_w_out):
    h = x
    for i in range(DEPTH):
        j = i // N_MIXERS
        hn = rmsnorm(h, norm_mix[i])
        if i % N_MIXERS == 0:
            mix = retention_mixer(hn, positions, ret_w_in[j], ret_gn_gain[j], ret_w_out[j])
        else:
            mix = gdn_mixer(hn, gdn_w_in[j], gdn_conv[j], gdn_a_log[j], gdn_dt_bias[j],
                            gdn_norm_gain[j], gdn_w_out[j])
        h = h + mix
        h = h + swiglu(rmsnorm(h, norm_ffn[i]), ffn_w_in[i], ffn_w_out[i])
    return rmsnorm(h, norm_final)
```

```python
import functools

import jax
import jax.numpy as jnp
from jax import lax
from jax.experimental import pallas as pl
from jax.experimental.pallas import tpu as pltpu

F32 = jnp.float32
BF16 = jnp.bfloat16
HIGHEST = lax.Precision.HIGHEST

D_MODEL = 1024
DEPTH = 4
N_MIXERS = 2

RET_HEADS = 4
RET_HEAD_QK = D_MODEL // RET_HEADS
RET_HEAD_V = 2 * RET_HEAD_QK
RET_QK = RET_HEADS * RET_HEAD_QK
RET_V = RET_HEADS * RET_HEAD_V
RET_IN = 2 * RET_QK + 2 * RET_V
RET_CHUNK = 128
ROPE_THETA = 10000.0

GDN_HEAD = 128
GDN_K_HEADS = D_MODEL // GDN_HEAD
GDN_V_HEADS = 2 * GDN_K_HEADS
GDN_QK = GDN_K_HEADS * GDN_HEAD
GDN_V = GDN_V_HEADS * GDN_HEAD
GDN_CONV_DIM = 2 * GDN_QK + GDN_V
GDN_MAIN = GDN_CONV_DIM + GDN_V
GDN_CONV_K = 4
GDN_CHUNK = 64
GDN_INV_BLOCK = 16

FFN_HIDDEN = 2816
NORM_EPS = 1e-6

VMEM_LIMIT = 48 * 1024 * 1024


def _params(*sem):
    return pltpu.CompilerParams(dimension_semantics=sem, vmem_limit_bytes=VMEM_LIMIT)


def _bdot(a, b):
    return jnp.dot(a.astype(BF16), b.astype(BF16), preferred_element_type=F32)


def _bdot_nt(a, b):
    return lax.dot_general(a.astype(BF16), b.astype(BF16), (((1,), (1,)), ((), ())),
                           preferred_element_type=F32)


def _bdot_tn(a, b):
    return lax.dot_general(a.astype(BF16), b.astype(BF16), (((0,), (0,)), ((), ())),
                           preferred_element_type=F32)


def _hdot(a, b):
    return jnp.dot(a, b, precision=HIGHEST, preferred_element_type=F32)


def _sigmoid(x):
    return 1.0 / (1.0 + jnp.exp(-x))


def _silu(x):
    return x * _sigmoid(x)


def _rmsnorm_rows(x, gain):
    return x * lax.rsqrt(jnp.mean(x * x, axis=-1, keepdims=True) + NORM_EPS) * gain


def _norm_proj_kernel(x_ref, g_ref, w_ref, o_ref, xn_ref):
    @pl.when(pl.program_id(1) == 0)
    def _():
        xn_ref[...] = _rmsnorm_rows(x_ref[...], g_ref[...]).astype(BF16)

    o_ref[...] = jnp.dot(xn_ref[...], w_ref[...], preferred_element_type=F32)


def _norm_proj_small_kernel(x_ref, g_ref, w_ref, ws_ref, o_ref, os_ref, xn_ref):
    @pl.when(pl.program_id(1) == 0)
    def _():
        xn_ref[...] = _rmsnorm_rows(x_ref[...], g_ref[...]).astype(BF16)
        os_ref[...] = jnp.dot(xn_ref[...], ws_ref[...], preferred_element_type=F32)

    o_ref[...] = jnp.dot(xn_ref[...], w_ref[...], preferred_element_type=F32)


def norm_proj(x, gain, w, w_small=None, *, tm=1024, tn=1536):
    m, d = x.shape
    n = w.shape[1]
    grid = (m // tm, n // tn)
    x_spec = pl.BlockSpec((tm, d), lambda i, j: (i, 0))
    g_spec = pl.BlockSpec((1, d), lambda i, j: (0, 0))
    w_spec = pl.BlockSpec((d, tn), lambda i, j: (0, j))
    o_spec = pl.BlockSpec((tm, tn), lambda i, j: (i, j))
    scratch = [pltpu.VMEM((tm, d), BF16)]
    if w_small is None:
        return pl.pallas_call(
            _norm_proj_kernel, grid=grid,
            in_specs=[x_spec, g_spec, w_spec], out_specs=o_spec,
            out_shape=jax.ShapeDtypeStruct((m, n), F32),
            scratch_shapes=scratch, compiler_params=_params("parallel", "arbitrary"),
            name="norm_proj",
        )(x, gain, w)
    ns = w_small.shape[1]
    return pl.pallas_call(
        _norm_proj_small_kernel, grid=grid,
        in_specs=[x_spec, g_spec, w_spec, pl.BlockSpec((d, ns), lambda i, j: (0, 0))],
        out_specs=[o_spec, pl.BlockSpec((tm, ns), lambda i, j: (i, 0))],
        out_shape=[jax.ShapeDtypeStruct((m, n), F32), jax.ShapeDtypeStruct((m, ns), F32)],
        scratch_shapes=scratch, compiler_params=_params("parallel", "arbitrary"),
        name="norm_proj_small",
    )(x, gain, w, w_small)


def _rope_table_kernel(pos_ref, freq_ref, cos_ref, sin_ref):
    ang = pos_ref[0].astype(F32) * freq_ref[...]
    cos_ref[0] = jnp.cos(ang)
    sin_ref[0] = jnp.sin(ang)


def rope_tables(positions, *, ts=512):
    b, s = positions.shape
    half = RET_HEAD_QK // 2
    inv_freq = (ROPE_THETA ** (-jnp.arange(half, dtype=F32) / half)).reshape(1, half)
    out = jax.ShapeDtypeStruct((b, s, half), F32)
    return pl.pallas_call(
        _rope_table_kernel, grid=(b, s // ts),
        in_specs=[pl.BlockSpec((1, ts, 1), lambda i, j: (i, j, 0)),
                  pl.BlockSpec((1, half), lambda i, j: (0, 0))],
        out_specs=[pl.BlockSpec((1, ts, half), lambda i, j: (i, j, 0))] * 2,
        out_shape=[out, out], compiler_params=_params("parallel", "parallel"),
        name="rope_tables",
    )(positions.reshape(b, s, 1), inv_freq)


def _retention_kernel(q_ref, k_ref, v_ref, gate_ref, cos_ref, sin_ref, decay_ref, qdec_ref,
                      kdec_ref, cdec_ref, gain_ref, y_ref, state_ref, *, chunks):
    c = RET_CHUNK
    half = RET_HEAD_QK // 2

    @pl.when(pl.program_id(2) == 0)
    def _():
        state_ref[...] = jnp.zeros_like(state_ref)

    decay = decay_ref[0]
    qdec = qdec_ref[0]
    kdec = kdec_ref[0]
    cdec = cdec_ref[0][0:1, 0:1]
    gain = gain_ref[...]

    def rope(t, cos, sin):
        t1, t2 = t[:, :half], t[:, half:]
        return jnp.concatenate([t1 * cos - t2 * sin, t2 * cos + t1 * sin], axis=-1)

    for ci in range(chunks):
        rows = pl.ds(ci * c, c)
        cos = cos_ref[0, rows, :]
        sin = sin_ref[0, rows, :]
        q = rope(q_ref[0, rows, :], cos, sin)
        k = rope(k_ref[0, rows, :], cos, sin) * (RET_HEAD_QK ** -0.5)
        v = v_ref[0, rows, :].astype(BF16)
        state = state_ref[...]
        scores = _bdot_nt(q, k) * decay
        o = _bdot(scores, v) + _bdot(q * qdec, state)
        state_ref[...] = state * cdec + _bdot_tn(k * kdec, v)
        mu = jnp.mean(o, axis=-1, keepdims=True)
        var = jnp.mean(jnp.square(o - mu), axis=-1, keepdims=True)
        on = (o - mu) * lax.rsqrt(var + NORM_EPS) * gain
        y_ref[0, rows, :] = (_silu(gate_ref[0, rows, :]) * on).astype(y_ref.dtype)


def retention_core(proj, cos, sin, gn_gain, *, chunks=2):
    b, s, _ = proj.shape
    c = RET_CHUNK
    h = RET_HEADS
    rows = c * chunks
    log_gamma = jnp.log1p(-jnp.exp2(-5.0 - jnp.arange(h, dtype=F32)))
    pos = jnp.arange(c, dtype=F32)
    causal = pos[:, None] >= pos[None, :]
    diff = jnp.where(causal, pos[:, None] - pos[None, :], 0.0)
    decay = jnp.where(causal, jnp.exp(log_gamma[:, None, None] * diff), 0.0)
    qdec = jnp.exp(log_gamma[:, None] * (pos + 1.0))[:, :, None]
    kdec = jnp.exp(log_gamma[:, None] * (c - 1.0 - pos))[:, :, None]
    cdec = jnp.broadcast_to(jnp.exp(log_gamma * c)[:, None, None], (h, 8, 128))
    nq = RET_QK // RET_HEAD_QK
    nqv = 2 * RET_QK // RET_HEAD_V
    dk, dv = RET_HEAD_QK, RET_HEAD_V
    return pl.pallas_call(
        functools.partial(_retention_kernel, chunks=chunks),
        grid=(b, h, s // rows),
        in_specs=[
            pl.BlockSpec((1, rows, dk), lambda bi, hi, ni: (bi, ni, hi)),
            pl.BlockSpec((1, rows, dk), lambda bi, hi, ni: (bi, ni, nq + hi)),
            pl.BlockSpec((1, rows, dv), lambda bi, hi, ni: (bi, ni, nqv + hi)),
            pl.BlockSpec((1, rows, dv), lambda bi, hi, ni: (bi, ni, nqv + h + hi)),
            pl.BlockSpec((1, rows, dk // 2), lambda bi, hi, ni: (bi, ni, 0)),
            pl.BlockSpec((1, rows, dk // 2), lambda bi, hi, ni: (bi, ni, 0)),
            pl.BlockSpec((1, c, c), lambda bi, hi, ni: (hi, 0, 0)),
            pl.BlockSpec((1, c, 1), lambda bi, hi, ni: (hi, 0, 0)),
            pl.BlockSpec((1, c, 1), lambda bi, hi, ni: (hi, 0, 0)),
            pl.BlockSpec((1, 8, 128), lambda bi, hi, ni: (hi, 0, 0)),
            pl.BlockSpec((1, dv), lambda bi, hi, ni: (0, hi)),
        ],
        out_specs=pl.BlockSpec((1, rows, dv), lambda bi, hi, ni: (bi, ni, hi)),
        out_shape=jax.ShapeDtypeStruct((b, s, RET_V), BF16),
        scratch_shapes=[pltpu.VMEM((dk, dv), F32)],
        compiler_params=_params("parallel", "parallel", "arbitrary"),
        name="retention_core",
    )(proj, proj, proj, proj, cos, sin, decay, qdec, kdec, cdec, gn_gain.reshape(1, RET_V))


def _gdn_pre_kernel(x_ref, prev_ref, ba_ref, cw_ref, alog_ref, dtb_ref,
                    q_ref, k_ref, v_ref, beta_ref, g_ref):
    first = pl.program_id(1) == 0
    t = x_ref.shape[1]
    row8 = lax.broadcasted_iota(jnp.int32, (8, GDN_HEAD), 0)

    def conv_silu(col):
        cols = pl.ds(col * GDN_HEAD, GDN_HEAD)
        x = x_ref[0, :, cols]
        prev = jnp.where(first, 0.0, prev_ref[0, :, cols])
        w = cw_ref[:, cols]
        acc = x * w[GDN_CONV_K - 1:GDN_CONV_K, :]
        for sh in range(1, GDN_CONV_K):
            xs = pltpu.roll(x, sh, 0)
            ps = pltpu.roll(prev, sh, 0)
            head = jnp.where(row8 < sh, ps, xs[:8])
            shifted = jnp.concatenate([head, xs[8:]], axis=0)
            acc = acc + shifted * w[GDN_CONV_K - 1 - sh:GDN_CONV_K - sh, :]
        return _silu(acc)

    def l2n(y):
        return y * lax.rsqrt(jnp.sum(y * y, axis=-1, keepdims=True) + NORM_EPS)

    for hh in range(GDN_K_HEADS):
        cols = pl.ds(hh * GDN_HEAD, GDN_HEAD)
        q_ref[0, :, cols] = l2n(conv_silu(hh)) * (GDN_HEAD ** -0.5)
        k_ref[0, :, cols] = l2n(conv_silu(GDN_K_HEADS + hh))
    for hh in range(GDN_V_HEADS):
        v_ref[0, :, pl.ds(hh * GDN_HEAD, GDN_HEAD)] = conv_silu(2 * GDN_K_HEADS + hh)

    ba = ba_ref[0]
    beta_ref[0] = _sigmoid(ba[:, :GDN_V_HEADS])
    a = ba[:, GDN_V_HEADS:] + dtb_ref[...]
    softplus = jnp.maximum(a, 0.0) + jnp.log1p(jnp.exp(-jnp.abs(a)))
    g_ref[0] = -jnp.exp(alog_ref[...]) * softplus


def gdn_pre(proj, ba, conv_w, a_log, dt_bias, *, ts=256):
    b, s, _ = proj.shape
    hv = GDN_V_HEADS
    f = lambda shape: jax.ShapeDtypeStruct(shape, F32)
    tile = lambda w: pl.BlockSpec((1, ts, w), lambda bi, si: (bi, si, 0))
    return pl.pallas_call(
        _gdn_pre_kernel, grid=(b, s // ts),
        in_specs=[
            tile(GDN_CONV_DIM),
            pl.BlockSpec((1, 8, GDN_CONV_DIM), lambda bi, si: (bi, jnp.maximum(si * (ts // 8) - 1, 0), 0)),
            tile(2 * hv),
            pl.BlockSpec((GDN_CONV_K, GDN_CONV_DIM), lambda bi, si: (0, 0)),
            pl.BlockSpec((1, hv), lambda bi, si: (0, 0)),
            pl.BlockSpec((1, hv), lambda bi, si: (0, 0)),
        ],
        out_specs=[tile(GDN_QK), tile(GDN_QK), tile(GDN_V), tile(hv), tile(hv)],
        out_shape=[f((b, s, GDN_QK)), f((b, s, GDN_QK)), f((b, s, GDN_V)), f((b, s, hv)), f((b, s, hv))],
        compiler_params=_params("parallel", "parallel"),
        name="gdn_pre",
    )(proj, proj, ba, conv_w, a_log.reshape(1, hv), dt_bias.reshape(1, hv))


def _unit_lower_inverse(a, eye, in_block):
    d = jnp.where(in_block, a, 0.0)
    e = a - d
    p = eye - d
    dp = d
    for _ in range(3):
        dp = _hdot(dp, dp)
        p = p + _hdot(p, dp)
    n = _hdot(p, e)
    m = eye - n
    m = m + _hdot(m, _hdot(n, n))
    return _hdot(m, p)


def _gdn_core_kernel(q_ref, k_ref, v_ref, beta_ref, g_ref, z_ref, gain_ref, y_ref, state_ref):
    c = GDN_CHUNK
    dh = GDN_HEAD
    rep = GDN_V_HEADS // GDN_K_HEADS

    @pl.when(pl.program_id(1) == 0)
    def _():
        state_ref[...] = jnp.zeros_like(state_ref)

    ri = lax.broadcasted_iota(jnp.int32, (c, c), 0)
    ci = lax.broadcasted_iota(jnp.int32, (c, c), 1)
    causal = ri >= ci
    strict = ri > ci
    eye = (ri == ci).astype(F32)
    in_block = (ri // GDN_INV_BLOCK) == (ci // GDN_INV_BLOCK)

    gcum = _hdot(causal.astype(F32), g_ref[0])
    beta = beta_ref[0]
    gain = gain_ref[...]

    for hv in range(GDN_V_HEADS):
        hk = hv // rep
        q = q_ref[0, :, pl.ds(hk * dh, dh)]
        k = k_ref[0, :, pl.ds(hk * dh, dh)]
        v = v_ref[0, :, pl.ds(hv * dh, dh)]
        bcol = beta[:, hv:hv + 1]
        gcol = gcum[:, hv:hv + 1]
        grow = jnp.sum(gcol * eye, axis=0, keepdims=True)
        glast = gcol[c - 1:c, :]
        decay = jnp.where(causal, jnp.exp(jnp.where(causal, gcol - grow, 0.0)), 0.0)
        eg = jnp.exp(gcol)
        kb = k * bcol
        a = jnp.where(strict, _bdot_nt(kb, k) * decay, 0.0)
        t = _unit_lower_inverse(a, eye, in_block)
        sol = _hdot(t, jnp.concatenate([v * bcol, kb * eg], axis=-1))
        u, w = sol[:, :dh], sol[:, dh:]
        attn = jnp.where(causal, _bdot_nt(q, k) * decay, 0.0)
        state = state_ref[hv]
        v_new = u - _bdot(w, state)
        o = _bdot(q * eg, state) + _bdot(attn, v_new)
        state_ref[hv] = state * jnp.exp(glast) + _bdot_tn(k * jnp.exp(glast - gcol), v_new)
        on = o * lax.rsqrt(jnp.mean(o * o, axis=-1, keepdims=True) + NORM_EPS) * gain
        cols = pl.ds(hv * dh, dh)
        y_ref[0, :, cols] = (on * _silu(z_ref[0, :, cols])).astype(y_ref.dtype)


def gdn_core(q, k, v, beta, g, proj, norm_gain):
    b, s, _ = q.shape
    c = GDN_CHUNK
    hv = GDN_V_HEADS
    tile = lambda w, blk=0: pl.BlockSpec((1, c, w), lambda bi, ni: (bi, ni, blk))
    return pl.pallas_call(
        _gdn_core_kernel, grid=(b, s // c),
        in_specs=[tile(GDN_QK), tile(GDN_QK), tile(GDN_V), tile(hv), tile(hv),
                  tile(GDN_V, GDN_CONV_DIM // GDN_V),
                  pl.BlockSpec((1, GDN_HEAD), lambda bi, ni: (0, 0))],
        out_specs=tile(GDN_V),
        out_shape=jax.ShapeDtypeStruct((b, s, GDN_V), BF16),
        scratch_shapes=[pltpu.VMEM((hv, GDN_HEAD, GDN_HEAD), F32)],
        compiler_params=_params("parallel", "arbitrary"),
        name="gdn_core",
    )(q, k, v, beta, g, proj, norm_gain.reshape(1, GDN_HEAD))


def _out_proj_kernel(h_ref, y_ref, w_ref, o_ref):
    o_ref[...] = h_ref[...] + jnp.dot(y_ref[...], w_ref[...], preferred_element_type=F32)


def out_proj_residual(h, y, w, *, tm=512):
    m, d = h.shape
    kdim = y.shape[1]
    return pl.pallas_call(
        _out_proj_kernel, grid=(m // tm,),
        in_specs=[pl.BlockSpec((tm, d), lambda i: (i, 0)),
                  pl.BlockSpec((tm, kdim), lambda i: (i, 0)),
                  pl.BlockSpec((kdim, d), lambda i: (0, 0))],
        out_specs=pl.BlockSpec((tm, d), lambda i: (i, 0)),
        out_shape=jax.ShapeDtypeStruct((m, d), F32),
        compiler_params=_params("parallel"),
        name="out_proj_residual",
    )(h, y, w)


def _ffn_kernel(h_ref, g_ref, win_ref, wout_ref, fg_ref, o_ref, *, hidden_chunk, final_norm):
    x = h_ref[...]
    xn = _rmsnorm_rows(x, g_ref[...]).astype(BF16)
    acc = x
    for c0 in range(0, FFN_HIDDEN, hidden_chunk):
        gate = jnp.dot(xn, win_ref[:, pl.ds(c0, hidden_chunk)], preferred_element_type=F32)
        up = jnp.dot(xn, win_ref[:, pl.ds(FFN_HIDDEN + c0, hidden_chunk)], preferred_element_type=F32)
        act = (_silu(gate) * up).astype(BF16)
        acc = acc + jnp.dot(act, wout_ref[pl.ds(c0, hidden_chunk), :], preferred_element_type=F32)
    if final_norm:
        acc = _rmsnorm_rows(acc, fg_ref[...])
    o_ref[...] = acc


def ffn_residual(h, gain, w_in, w_out, final_gain, *, final_norm, tm=256, hidden_chunk=1408):
    m, d = h.shape
    return pl.pallas_call(
        functools.partial(_ffn_kernel, hidden_chunk=hidden_chunk, final_norm=final_norm),
        grid=(m // tm,),
        in_specs=[pl.BlockSpec((tm, d), lambda i: (i, 0)),
                  pl.BlockSpec((1, d), lambda i: (0, 0)),
                  pl.BlockSpec((d, 2 * FFN_HIDDEN), lambda i: (0, 0)),
                  pl.BlockSpec((FFN_HIDDEN, d), lambda i: (0, 0)),
                  pl.BlockSpec((1, d), lambda i: (0, 0))],
        out_specs=pl.BlockSpec((tm, d), lambda i: (i, 0)),
        out_shape=jax.ShapeDtypeStruct((m, d), F32),
        compiler_params=_params("parallel"),
        name="ffn_residual",
    )(h, gain, w_in, w_out, final_gain)


def kernel(x, positions, norm_mix, norm_ffn, norm_final, ret_w_in, ret_gn_gain, ret_w_out,
           gdn_w_in, gdn_conv, gdn_a_log, gdn_dt_bias, gdn_norm_gain, gdn_w_out,
           ffn_w_in, ffn_w_out):
    b, s, d = x.shape
    m = b * s
    h = x.reshape(m, d)
    cos, sin = rope_tables(positions)
    final_gain = norm_final.reshape(1, d)
    for i in range(DEPTH):
        j = i // N_MIXERS
        gain = norm_mix[i].reshape(1, d)
        if i % N_MIXERS == 0:
            proj = norm_proj(h, gain, ret_w_in[j].astype(BF16))
            y = retention_core(proj.reshape(b, s, RET_IN), cos, sin, ret_gn_gain[j])
            w_out = ret_w_out[j]
        else:
            w_in = gdn_w_in[j]
            proj, ba = norm_proj(h, gain, w_in[:, :GDN_MAIN].astype(BF16), w_in[:, GDN_MAIN:].astype(BF16))
            proj = proj.reshape(b, s, GDN_MAIN)
            q, k, v, beta, g = gdn_pre(proj, ba.reshape(b, s, -1), gdn_conv[j], gdn_a_log[j], gdn_dt_bias[j])
            y = gdn_core(q, k, v, beta, g, proj, gdn_norm_gain[j])
            w_out = gdn_w_out[j]
        h = out_proj_residual(h, y.reshape(m, -1), w_out.astype(BF16))
        h = ffn_residual(h, norm_ffn[i].reshape(1, d), ffn_w_in[i].astype(BF16), ffn_w_out[i].astype(BF16),
                         final_gain, final_norm=(i == DEPTH - 1))
    return h.reshape(b, s, d)
```

```python
import functools

import jax
import jax.numpy as jnp
from jax import lax
from jax.experimental import pallas as pl
from jax.experimental.pallas import tpu as pltpu

F32 = jnp.float32
BF16 = jnp.bfloat16
HIGHEST = lax.Precision.HIGHEST

D_MODEL = 1024
DEPTH = 4
N_MIXERS = 2

RET_HEADS = 4
RET_HEAD_QK = D_MODEL // RET_HEADS
RET_HEAD_V = 2 * RET_HEAD_QK
RET_QK = RET_HEADS * RET_HEAD_QK
RET_V = RET_HEADS * RET_HEAD_V
RET_IN = 2 * RET_QK + 2 * RET_V
RET_CHUNK = 128
ROPE_THETA = 10000.0

GDN_HEAD = 128
GDN_K_HEADS = D_MODEL // GDN_HEAD
GDN_V_HEADS = 2 * GDN_K_HEADS
GDN_QK = GDN_K_HEADS * GDN_HEAD
GDN_V = GDN_V_HEADS * GDN_HEAD
GDN_CONV_DIM = 2 * GDN_QK + GDN_V
GDN_MAIN = GDN_CONV_DIM + GDN_V
GDN_CONV_K = 4
GDN_CHUNK = 64
GDN_INV_BLOCK = 8

FFN_HIDDEN = 2816
NORM_EPS = 1e-6

VMEM_LIMIT = 48 * 1024 * 1024


def _params(*sem):
    return pltpu.CompilerParams(dimension_semantics=sem, vmem_limit_bytes=VMEM_LIMIT)


def _bdot(a, b):
    return jnp.dot(a.astype(BF16), b.astype(BF16), preferred_element_type=F32)


def _bdot_nt(a, b):
    return lax.dot_general(a.astype(BF16), b.astype(BF16), (((1,), (1,)), ((), ())),
                           preferred_element_type=F32)


def _bdot_tn(a, b):
    return lax.dot_general(a.astype(BF16), b.astype(BF16), (((0,), (0,)), ((), ())),
                           preferred_element_type=F32)


def _hdot(a, b):
    return jnp.dot(a, b, precision=HIGHEST, preferred_element_type=F32)


def _sigmoid(x):
    return 1.0 / (1.0 + jnp.exp(-x))


def _silu(x):
    return x * _sigmoid(x)


def _rmsnorm_rows(x, gain):
    return x * lax.rsqrt(jnp.mean(x * x, axis=-1, keepdims=True) + NORM_EPS) * gain


def _norm_proj_kernel(x_ref, g_ref, w_ref, o_ref, xn_ref):
    @pl.when(pl.program_id(1) == 0)
    def _():
        xn_ref[...] = _rmsnorm_rows(x_ref[...], g_ref[...]).astype(BF16)

    o_ref[...] = jnp.dot(xn_ref[...], w_ref[...], preferred_element_type=F32)


def _norm_proj_small_kernel(x_ref, g_ref, w_ref, ws_ref, o_ref, os_ref, xn_ref):
    @pl.when(pl.program_id(1) == 0)
    def _():
        xn_ref[...] = _rmsnorm_rows(x_ref[...], g_ref[...]).astype(BF16)
        os_ref[...] = jnp.dot(xn_ref[...], ws_ref[...], preferred_element_type=F32)

    o_ref[...] = jnp.dot(xn_ref[...], w_ref[...], preferred_element_type=F32)


def norm_proj(x, gain, w, w_small=None, *, tm=1024, tn=1536):
    m, d = x.shape
    n = w.shape[1]
    grid = (m // tm, n // tn)
    x_spec = pl.BlockSpec((tm, d), lambda i, j: (i, 0))
    g_spec = pl.BlockSpec((1, d), lambda i, j: (0, 0))
    w_spec = pl.BlockSpec((d, tn), lambda i, j: (0, j))
    o_spec = pl.BlockSpec((tm, tn), lambda i, j: (i, j))
    scratch = [pltpu.VMEM((tm, d), BF16)]
    if w_small is None:
        return pl.pallas_call(
            _norm_proj_kernel, grid=grid,
            in_specs=[x_spec, g_spec, w_spec], out_specs=o_spec,
            out_shape=jax.ShapeDtypeStruct((m, n), F32),
            scratch_shapes=scratch, compiler_params=_params("parallel", "arbitrary"),
            name="norm_proj",
        )(x, gain, w)
    ns = w_small.shape[1]
    return pl.pallas_call(
        _norm_proj_small_kernel, grid=grid,
        in_specs=[x_spec, g_spec, w_spec, pl.BlockSpec((d, ns), lambda i, j: (0, 0))],
        out_specs=[o_spec, pl.BlockSpec((tm, ns), lambda i, j: (i, 0))],
        out_shape=[jax.ShapeDtypeStruct((m, n), F32), jax.ShapeDtypeStruct((m, ns), F32)],
        scratch_shapes=scratch, compiler_params=_params("parallel", "arbitrary"),
        name="norm_proj_small",
    )(x, gain, w, w_small)


def _rope_table_kernel(pos_ref, freq_ref, cos_ref, sin_ref):
    ang = pos_ref[0].astype(F32) * freq_ref[...]
    cos_ref[0] = jnp.cos(ang)
    sin_ref[0] = jnp.sin(ang)


def rope_tables(positions, *, ts=512):
    b, s = positions.shape
    half = RET_HEAD_QK // 2
    inv_freq = (ROPE_THETA ** (-jnp.arange(half, dtype=F32) / half)).reshape(1, half)
    out = jax.ShapeDtypeStruct((b, s, half), F32)
    return pl.pallas_call(
        _rope_table_kernel, grid=(b, s // ts),
        in_specs=[pl.BlockSpec((1, ts, 1), lambda i, j: (i, j, 0)),
                  pl.BlockSpec((1, half), lambda i, j: (0, 0))],
        out_specs=[pl.BlockSpec((1, ts, half), lambda i, j: (i, j, 0))] * 2,
        out_shape=[out, out], compiler_params=_params("parallel", "parallel"),
        name="rope_tables",
    )(positions.reshape(b, s, 1), inv_freq)


def _retention_kernel(q_ref, k_ref, v_ref, gate_ref, cos_ref, sin_ref, decay_ref, qdec_ref,
                      kdec_ref, cdec_ref, gain_ref, y_ref, state_ref, *, chunks):
    c = RET_CHUNK
    dk, dv = RET_HEAD_QK, RET_HEAD_V
    half = dk // 2
    heads = range(RET_HEADS)

    @pl.when(pl.program_id(1) == 0)
    def _():
        state_ref[...] = jnp.zeros_like(state_ref)

    def rope(t, cos, sin):
        t1, t2 = t[:, :half], t[:, half:]
        return jnp.concatenate([t1 * cos - t2 * sin, t2 * cos + t1 * sin], axis=-1)

    qd, inner, upd = [], [], []
    for ci in range(chunks):
        rows = pl.ds(ci * c, c)
        cos = cos_ref[0, rows, :]
        sin = sin_ref[0, rows, :]
        for h in heads:
            q = rope(q_ref[0, rows, pl.ds(h * dk, dk)], cos, sin)
            k = rope(k_ref[0, rows, pl.ds(h * dk, dk)], cos, sin) * (dk ** -0.5)
            v = v_ref[0, rows, pl.ds(h * dv, dv)].astype(BF16)
            scores = _mm_nt(q.astype(BF16), k.astype(BF16)) * decay_ref[h]
            inner.append(_mm(scores.astype(BF16), v))
            upd.append(_mm_tn((k * kdec_ref[h]).astype(BF16), v))
            qd.append((q * qdec_ref[h]).astype(BF16))

    for ci in range(chunks):
        rows = pl.ds(ci * c, c)
        for h in heads:
            p = ci * RET_HEADS + h
            state = state_ref[h]
            o = inner[p] + _mm(qd[p], state.astype(BF16))
            state_ref[h] = state * cdec_ref[h][0:1, 0:1] + upd[p]
            mu = jnp.mean(o, axis=-1, keepdims=True)
            var = jnp.mean(jnp.square(o - mu), axis=-1, keepdims=True)
            cols = pl.ds(h * dv, dv)
            on = (o - mu) * lax.rsqrt(var + NORM_EPS) * gain_ref[:, cols]
            y_ref[0, rows, cols] = (_silu(gate_ref[0, rows, cols]) * on).astype(y_ref.dtype)


def retention_core(proj, cos, sin, gn_gain, *, chunks=4):
    b, s, _ = proj.shape
    c = RET_CHUNK
    h = RET_HEADS
    rows = c * chunks
    log_gamma = jnp.log1p(-jnp.exp2(-5.0 - jnp.arange(h, dtype=F32)))
    pos = jnp.arange(c, dtype=F32)
    causal = pos[:, None] >= pos[None, :]
    diff = jnp.where(causal, pos[:, None] - pos[None, :], 0.0)
    decay = jnp.where(causal, jnp.exp(log_gamma[:, None, None] * diff), 0.0)
    qdec = jnp.exp(log_gamma[:, None] * (pos + 1.0))[:, :, None]
    kdec = jnp.exp(log_gamma[:, None] * (c - 1.0 - pos))[:, :, None]
    cdec = jnp.broadcast_to(jnp.exp(log_gamma * c)[:, None, None], (h, 8, 128))
    dk, dv = RET_HEAD_QK, RET_HEAD_V
    tile = lambda w, blk: pl.BlockSpec((1, rows, w), lambda bi, ni: (bi, ni, blk))
    whole = lambda *shape: pl.BlockSpec(shape, lambda bi, ni: (0,) * len(shape))
    return pl.pallas_call(
        functools.partial(_retention_kernel, chunks=chunks),
        grid=(b, s // rows),
        in_specs=[
            tile(RET_QK, 0), tile(RET_QK, 1), tile(RET_V, 1), tile(RET_V, 2),
            tile(dk // 2, 0), tile(dk // 2, 0),
            whole(h, c, c), whole(h, c, 1), whole(h, c, 1), whole(h, 8, 128), whole(1, RET_V),
        ],
        out_specs=tile(RET_V, 0),
        out_shape=jax.ShapeDtypeStruct((b, s, RET_V), BF16),
        scratch_shapes=[pltpu.VMEM((h, dk, dv), F32)],
        compiler_params=_params("parallel", "arbitrary"),
        name="retention_core",
    )(proj, proj, proj, proj, cos, sin, decay, qdec, kdec, cdec, gn_gain.reshape(1, RET_V))


def _gdn_pre_kernel(x_ref, prev_ref, ba_ref, cw_ref, alog_ref, dtb_ref,
                    q_ref, k_ref, v_ref, beta_ref, g_ref):
    first = pl.program_id(1) == 0
    t = x_ref.shape[1]
    row8 = lax.broadcasted_iota(jnp.int32, (8, GDN_HEAD), 0)

    def conv_silu(col):
        cols = pl.ds(col * GDN_HEAD, GDN_HEAD)
        x = x_ref[0, :, cols]
        prev = jnp.where(first, 0.0, prev_ref[0, :, cols])
        w = cw_ref[:, cols]
        acc = x * w[GDN_CONV_K - 1:GDN_CONV_K, :]
        for sh in range(1, GDN_CONV_K):
            xs = pltpu.roll(x, sh, 0)
            ps = pltpu.roll(prev, sh, 0)
            head = jnp.where(row8 < sh, ps, xs[:8])
            shifted = jnp.concatenate([head, xs[8:]], axis=0)
            acc = acc + shifted * w[GDN_CONV_K - 1 - sh:GDN_CONV_K - sh, :]
        return _silu(acc)

    def l2n(y):
        return y * lax.rsqrt(jnp.sum(y * y, axis=-1, keepdims=True) + NORM_EPS)

    for hh in range(GDN_K_HEADS):
        cols = pl.ds(hh * GDN_HEAD, GDN_HEAD)
        q_ref[0, :, cols] = l2n(conv_silu(hh)) * (GDN_HEAD ** -0.5)
        k_ref[0, :, cols] = l2n(conv_silu(GDN_K_HEADS + hh))
    for hh in range(GDN_V_HEADS):
        v_ref[0, :, pl.ds(hh * GDN_HEAD, GDN_HEAD)] = conv_silu(2 * GDN_K_HEADS + hh)

    ba = ba_ref[0]
    beta_ref[0] = _sigmoid(ba[:, :GDN_V_HEADS])
    a = ba[:, GDN_V_HEADS:] + dtb_ref[...]
    softplus = jnp.maximum(a, 0.0) + jnp.log1p(jnp.exp(-jnp.abs(a)))
    g_ref[0] = -jnp.exp(alog_ref[...]) * softplus


def gdn_pre(proj, ba, conv_w, a_log, dt_bias, *, ts=256):
    b, s, _ = proj.shape
    hv = GDN_V_HEADS
    f = lambda shape: jax.ShapeDtypeStruct(shape, F32)
    tile = lambda w: pl.BlockSpec((1, ts, w), lambda bi, si: (bi, si, 0))
    return pl.pallas_call(
        _gdn_pre_kernel, grid=(b, s // ts),
        in_specs=[
            tile(GDN_CONV_DIM),
            pl.BlockSpec((1, 8, GDN_CONV_DIM), lambda bi, si: (bi, jnp.maximum(si * (ts // 8) - 1, 0), 0)),
            tile(2 * hv),
            pl.BlockSpec((GDN_CONV_K, GDN_CONV_DIM), lambda bi, si: (0, 0)),
            pl.BlockSpec((1, hv), lambda bi, si: (0, 0)),
            pl.BlockSpec((1, hv), lambda bi, si: (0, 0)),
        ],
        out_specs=[tile(GDN_QK), tile(GDN_QK), tile(GDN_V), tile(hv), tile(hv)],
        out_shape=[f((b, s, GDN_QK)), f((b, s, GDN_QK)), f((b, s, GDN_V)), f((b, s, hv)), f((b, s, hv))],
        compiler_params=_params("parallel", "parallel"),
        name="gdn_pre",
    )(proj, proj, ba, conv_w, a_log.reshape(1, hv), dt_bias.reshape(1, hv))


def _mm(a, b):
    return jnp.dot(a, b, preferred_element_type=F32)


def _mm_nt(a, b):
    return lax.dot_general(a, b, (((1,), (1,)), ((), ())), preferred_element_type=F32)


def _mm_tn(a, b):
    return lax.dot_general(a, b, (((0,), (0,)), ((), ())), preferred_element_type=F32)


def _bf(xs):
    return [x.astype(BF16) for x in xs]


def _unit_lower_inverses(a_list, eye, base_mask, level_masks):
    d = [jnp.where(base_mask, a, 0.0) for a in a_list]
    db = _bf(d)
    d2 = [_mm(x, x) for x in db]
    d2b = _bf(d2)
    p = [eye - x for x in d]
    p = [pi + _mm(pi.astype(BF16), x2) for pi, x2 in zip(p, d2b)]
    d4b = _bf([_mm(x, x) for x in d2b])
    t = [pi + _mm(pi.astype(BF16), x4) for pi, x4 in zip(p, d4b)]
    for mask in level_masks:
        eb = _bf([jnp.where(mask, a, 0.0) for a in a_list])
        tb = _bf(t)
        teb = _bf([_mm(ti, ei) for ti, ei in zip(tb, eb)])
        t = [ti - _mm(tei, tbi) for ti, tei, tbi in zip(t, teb, tb)]
    return t


def _gdn_core_kernel(q_ref, k_ref, v_ref, beta_ref, g_ref, z_ref, gain_ref, y_ref, state_ref, *, chunks):
    c = GDN_CHUNK
    dh = GDN_HEAD
    nk = GDN_K_HEADS
    nv = GDN_V_HEADS
    rep = nv // nk

    @pl.when(pl.program_id(1) == 0)
    def _():
        state_ref[...] = jnp.zeros_like(state_ref)

    ri = lax.broadcasted_iota(jnp.int32, (c, c), 0)
    ci = lax.broadcasted_iota(jnp.int32, (c, c), 1)
    causal = ri >= ci
    strict = ri > ci
    eye = (ri == ci).astype(F32)
    tril = causal.astype(F32)
    same = lambda size: (ri // size) == (ci // size)
    base_mask = same(GDN_INV_BLOCK)
    level_masks = []
    size = GDN_INV_BLOCK
    while size < c:
        level_masks.append(same(2 * size) & jnp.logical_not(same(size)))
        size *= 2
    gain = gain_ref[...]

    def chunk_step(idx, carry):
        rows = pl.ds(pl.multiple_of(idx * c, c), c)
        heads = range(nv)
        col = lambda x, h: x[:, h:h + 1]

        gcum = _hdot(tril, g_ref[0, rows, :])
        beta = beta_ref[0, rows, :]
        glast = gcum[c - 1:c, :]
        eg = jnp.exp(gcum)
        kdec = jnp.exp(glast - gcum)
        eglast = jnp.exp(glast)

        q = [q_ref[0, rows, pl.ds(h * dh, dh)] for h in range(nk)]
        k = [k_ref[0, rows, pl.ds(h * dh, dh)] for h in range(nk)]
        kb16 = _bf(k)
        kk = [_mm_nt(x, x) for x in kb16]
        qk = [_mm_nt(x, y) for x, y in zip(_bf(q), kb16)]

        grow = [jnp.sum(col(gcum, h) * eye, axis=0, keepdims=True) for h in heads]
        decay = [jnp.where(causal, jnp.exp(jnp.where(causal, col(gcum, h) - grow[h], 0.0)), 0.0)
                 for h in heads]
        a = [jnp.where(strict, kk[h // rep] * col(beta, h) * decay[h], 0.0) for h in heads]
        t = _unit_lower_inverses(a, eye, base_mask, level_masks)

        v = [v_ref[0, rows, pl.ds(h * dh, dh)] for h in heads]
        rhs = [jnp.concatenate([v[h] * col(beta, h), k[h // rep] * (col(beta, h) * col(eg, h))],
                               axis=-1).astype(BF16) for h in heads]
        sol = [_mm(ti, ri_) for ti, ri_ in zip(_bf(t), rhs)]
        attn = _bf([jnp.where(causal, qk[h // rep] * decay[h], 0.0) for h in heads])

        state = [state_ref[h] for h in heads]
        sb = _bf(state)
        wq = [jnp.concatenate([sol[h][:, dh:], q[h // rep] * col(eg, h)], axis=0).astype(BF16) for h in heads]
        ws_qs = [_mm(x, s) for x, s in zip(wq, sb)]
        v_new = _bf([sol[h][:, :dh] - ws_qs[h][:c] for h in heads])
        o = [ws_qs[h][c:] + _mm(attn[h], v_new[h]) for h in heads]
        kd = _bf([k[h // rep] * col(kdec, h) for h in heads])
        for h in heads:
            state_ref[h] = state[h] * eglast[:, h:h + 1] + _mm_tn(kd[h], v_new[h])
        for h in heads:
            on = o[h] * lax.rsqrt(jnp.mean(o[h] * o[h], axis=-1, keepdims=True) + NORM_EPS) * gain
            cols = pl.ds(h * dh, dh)
            y_ref[0, rows, cols] = (on * _silu(z_ref[0, rows, cols])).astype(y_ref.dtype)
        return carry

    lax.fori_loop(0, chunks, chunk_step, 0)


def gdn_core(q, k, v, beta, g, proj, norm_gain, *, chunks=4):
    b, s, _ = q.shape
    rows = GDN_CHUNK * chunks
    hv = GDN_V_HEADS
    tile = lambda w, blk=0: pl.BlockSpec((1, rows, w), lambda bi, ni: (bi, ni, blk))
    return pl.pallas_call(
        functools.partial(_gdn_core_kernel, chunks=chunks), grid=(b, s // rows),
        in_specs=[tile(GDN_QK), tile(GDN_QK), tile(GDN_V), tile(hv), tile(hv),
                  tile(GDN_V, GDN_CONV_DIM // GDN_V),
                  pl.BlockSpec((1, GDN_HEAD), lambda bi, ni: (0, 0))],
        out_specs=tile(GDN_V),
        out_shape=jax.ShapeDtypeStruct((b, s, GDN_V), BF16),
        scratch_shapes=[pltpu.VMEM((hv, GDN_HEAD, GDN_HEAD), F32)],
        compiler_params=_params("parallel", "arbitrary"),
        name="gdn_core",
    )(q, k, v, beta, g, proj, norm_gain.reshape(1, GDN_HEAD))


def _out_proj_kernel(h_ref, y_ref, w_ref, o_ref):
    o_ref[...] = h_ref[...] + jnp.dot(y_ref[...], w_ref[...], preferred_element_type=F32)


def out_proj_residual(h, y, w, *, tm=512):
    m, d = h.shape
    kdim = y.shape[1]
    return pl.pallas_call(
        _out_proj_kernel, grid=(m // tm,),
        in_specs=[pl.BlockSpec((tm, d), lambda i: (i, 0)),
                  pl.BlockSpec((tm, kdim), lambda i: (i, 0)),
                  pl.BlockSpec((kdim, d), lambda i: (0, 0))],
        out_specs=pl.BlockSpec((tm, d), lambda i: (i, 0)),
        out_shape=jax.ShapeDtypeStruct((m, d), F32),
        compiler_params=_params("parallel"),
        name="out_proj_residual",
    )(h, y, w)


def _ffn_kernel(h_ref, g_ref, win_ref, wout_ref, fg_ref, o_ref, *, hidden_chunk, final_norm):
    x = h_ref[...]
    xn = _rmsnorm_rows(x, g_ref[...]).astype(BF16)
    acc = x
    for c0 in range(0, FFN_HIDDEN, hidden_chunk):
        gate = jnp.dot(xn, win_ref[:, pl.ds(c0, hidden_chunk)], preferred_element_type=F32)
        up = jnp.dot(xn, win_ref[:, pl.ds(FFN_HIDDEN + c0, hidden_chunk)], preferred_element_type=F32)
        act = (_silu(gate) * up).astype(BF16)
        acc = acc + jnp.dot(act, wout_ref[pl.ds(c0, hidden_chunk), :], preferred_element_type=F32)
    if final_norm:
        acc = _rmsnorm_rows(acc, fg_ref[...])
    o_ref[...] = acc


def ffn_residual(h, gain, w_in, w_out, final_gain, *, final_norm, tm=256, hidden_chunk=1408):
    m, d = h.shape
    return pl.pallas_call(
        functools.partial(_ffn_kernel, hidden_chunk=hidden_chunk, final_norm=final_norm),
        grid=(m // tm,),
        in_specs=[pl.BlockSpec((tm, d), lambda i: (i, 0)),
                  pl.BlockSpec((1, d), lambda i: (0, 0)),
                  pl.BlockSpec((d, 2 * FFN_HIDDEN), lambda i: (0, 0)),
                  pl.BlockSpec((FFN_HIDDEN, d), lambda i: (0, 0)),
                  pl.BlockSpec((1, d), lambda i: (0, 0))],
        out_specs=pl.BlockSpec((tm, d), lambda i: (i, 0)),
        out_shape=jax.ShapeDtypeStruct((m, d), F32),
        compiler_params=_params("parallel"),
        name="ffn_residual",
    )(h, gain, w_in, w_out, final_gain)


def kernel(x, positions, norm_mix, norm_ffn, norm_final, ret_w_in, ret_gn_gain, ret_w_out,
           gdn_w_in, gdn_conv, gdn_a_log, gdn_dt_bias, gdn_norm_gain, gdn_w_out,
           ffn_w_in, ffn_w_out):
    b, s, d = x.shape
    m = b * s
    h = x.reshape(m, d)
    cos, sin = rope_tables(positions)
    final_gain = norm_final.reshape(1, d)
    for i in range(DEPTH):
        j = i // N_MIXERS
        gain = norm_mix[i].reshape(1, d)
        if i % N_MIXERS == 0:
            proj = norm_proj(h, gain, ret_w_in[j].astype(BF16))
            y = retention_core(proj.reshape(b, s, RET_IN), cos, sin, ret_gn_gain[j])
            w_out = ret_w_out[j]
        else:
            w_in = gdn_w_in[j]
            proj, ba = norm_proj(h, gain, w_in[:, :GDN_MAIN].astype(BF16), w_in[:, GDN_MAIN:].astype(BF16))
            proj = proj.reshape(b, s, GDN_MAIN)
            q, k, v, beta, g = gdn_pre(proj, ba.reshape(b, s, -1), gdn_conv[j], gdn_a_log[j], gdn_dt_bias[j])
            y = gdn_core(q, k, v, beta, g, proj, gdn_norm_gain[j])
            w_out = gdn_w_out[j]
        h = out_proj_residual(h, y.reshape(m, -1), w_out.astype(BF16))
        h = ffn_residual(h, norm_ffn[i].reshape(1, d), ffn_w_in[i].astype(BF16), ffn_w_out[i].astype(BF16),
                         final_gain, final_norm=(i == DEPTH - 1))
    return h.reshape(b, s, d)
```

```python
import functools

import jax
import jax.numpy as jnp
from jax import lax
from jax.experimental import pallas as pl
from jax.experimental.pallas import tpu as pltpu

F32 = jnp.float32
BF16 = jnp.bfloat16
HIGHEST = lax.Precision.HIGHEST

D_MODEL = 1024
DEPTH = 4
N_MIXERS = 2

RET_HEADS = 4
RET_HEAD_QK = D_MODEL // RET_HEADS
RET_HEAD_V = 2 * RET_HEAD_QK
RET_QK = RET_HEADS * RET_HEAD_QK
RET_V = RET_HEADS * RET_HEAD_V
RET_IN = 2 * RET_QK + 2 * RET_V
RET_CHUNK = 128
ROPE_THETA = 10000.0

GDN_HEAD = 128
GDN_K_HEADS = D_MODEL // GDN_HEAD
GDN_V_HEADS = 2 * GDN_K_HEADS
GDN_QK = GDN_K_HEADS * GDN_HEAD
GDN_V = GDN_V_HEADS * GDN_HEAD
GDN_CONV_DIM = 2 * GDN_QK + GDN_V
GDN_MAIN = GDN_CONV_DIM + GDN_V
GDN_CONV_K = 4
GDN_CHUNK = 64
GDN_INV_BLOCK = 8
PREV_ROWS = 16

FFN_HIDDEN = 2816
NORM_EPS = 1e-6

VMEM_LIMIT = 48 * 1024 * 1024


def _params(*sem):
    return pltpu.CompilerParams(dimension_semantics=sem, vmem_limit_bytes=VMEM_LIMIT)


def _bdot(a, b):
    return jnp.dot(a.astype(BF16), b.astype(BF16), preferred_element_type=F32)


def _bdot_nt(a, b):
    return lax.dot_general(a.astype(BF16), b.astype(BF16), (((1,), (1,)), ((), ())),
                           preferred_element_type=F32)


def _bdot_tn(a, b):
    return lax.dot_general(a.astype(BF16), b.astype(BF16), (((0,), (0,)), ((), ())),
                           preferred_element_type=F32)


def _hdot(a, b):
    return jnp.dot(a, b, precision=HIGHEST, preferred_element_type=F32)


def _sigmoid(x):
    return 1.0 / (1.0 + jnp.exp(-x))


def _silu(x):
    return x * _sigmoid(x)


def _rmsnorm_rows(x, gain):
    return x * lax.rsqrt(jnp.mean(x * x, axis=-1, keepdims=True) + NORM_EPS) * gain


def _norm_proj_kernel(x_ref, g_ref, w_ref, o_ref, xn_ref):
    @pl.when(pl.program_id(1) == 0)
    def _():
        xn_ref[...] = _rmsnorm_rows(x_ref[...], g_ref[...]).astype(BF16)

    o_ref[...] = jnp.dot(xn_ref[...], w_ref[...], preferred_element_type=F32).astype(o_ref.dtype)


def _norm_proj_small_kernel(x_ref, g_ref, w_ref, ws_ref, o_ref, os_ref, xn_ref):
    @pl.when(pl.program_id(1) == 0)
    def _():
        xn_ref[...] = _rmsnorm_rows(x_ref[...], g_ref[...]).astype(BF16)
        os_ref[...] = jnp.dot(xn_ref[...], ws_ref[...], preferred_element_type=F32)

    o_ref[...] = jnp.dot(xn_ref[...], w_ref[...], preferred_element_type=F32).astype(o_ref.dtype)


def norm_proj(x, gain, w, w_small=None, *, tm=1024, tn=1536):
    m, d = x.shape
    n = w.shape[1]
    grid = (m // tm, n // tn)
    x_spec = pl.BlockSpec((tm, d), lambda i, j: (i, 0))
    g_spec = pl.BlockSpec((1, d), lambda i, j: (0, 0))
    w_spec = pl.BlockSpec((d, tn), lambda i, j: (0, j))
    o_spec = pl.BlockSpec((tm, tn), lambda i, j: (i, j))
    scratch = [pltpu.VMEM((tm, d), BF16)]
    if w_small is None:
        return pl.pallas_call(
            _norm_proj_kernel, grid=grid,
            in_specs=[x_spec, g_spec, w_spec], out_specs=o_spec,
            out_shape=jax.ShapeDtypeStruct((m, n), BF16),
            scratch_shapes=scratch, compiler_params=_params("parallel", "arbitrary"),
            name="norm_proj",
        )(x, gain, w)
    ns = w_small.shape[1]
    return pl.pallas_call(
        _norm_proj_small_kernel, grid=grid,
        in_specs=[x_spec, g_spec, w_spec, pl.BlockSpec((d, ns), lambda i, j: (0, 0))],
        out_specs=[o_spec, pl.BlockSpec((tm, ns), lambda i, j: (i, 0))],
        out_shape=[jax.ShapeDtypeStruct((m, n), BF16), jax.ShapeDtypeStruct((m, ns), F32)],
        scratch_shapes=scratch, compiler_params=_params("parallel", "arbitrary"),
        name="norm_proj_small",
    )(x, gain, w, w_small)


def _rope_table_kernel(pos_ref, freq_ref, cos_ref, sin_ref):
    ang = pos_ref[0].astype(F32) * freq_ref[...]
    cos_ref[0] = jnp.cos(ang)
    sin_ref[0] = jnp.sin(ang)


def rope_tables(positions, *, ts=512):
    b, s = positions.shape
    half = RET_HEAD_QK // 2
    inv_freq = (ROPE_THETA ** (-jnp.arange(half, dtype=F32) / half)).reshape(1, half)
    out = jax.ShapeDtypeStruct((b, s, half), F32)
    return pl.pallas_call(
        _rope_table_kernel, grid=(b, s // ts),
        in_specs=[pl.BlockSpec((1, ts, 1), lambda i, j: (i, j, 0)),
                  pl.BlockSpec((1, half), lambda i, j: (0, 0))],
        out_specs=[pl.BlockSpec((1, ts, half), lambda i, j: (i, j, 0))] * 2,
        out_shape=[out, out], compiler_params=_params("parallel", "parallel"),
        name="rope_tables",
    )(positions.reshape(b, s, 1), inv_freq)


def _retention_kernel(q_ref, k_ref, v_ref, gate_ref, cos_ref, sin_ref, decay_ref, qdec_ref,
                      kdec_ref, cdec_ref, gain_ref, y_ref, state_ref, *, chunks):
    c = RET_CHUNK
    dk, dv = RET_HEAD_QK, RET_HEAD_V
    half = dk // 2
    heads = range(RET_HEADS)

    @pl.when(pl.program_id(1) == 0)
    def _():
        state_ref[...] = jnp.zeros_like(state_ref)

    def rope(t, cos, sin):
        t1, t2 = t[:, :half], t[:, half:]
        return jnp.concatenate([t1 * cos - t2 * sin, t2 * cos + t1 * sin], axis=-1)

    qd, inner, upd = [], [], []
    for ci in range(chunks):
        rows = pl.ds(ci * c, c)
        cos = cos_ref[0, rows, :]
        sin = sin_ref[0, rows, :]
        for h in heads:
            q = rope(q_ref[0, rows, pl.ds(h * dk, dk)].astype(F32), cos, sin)
            k = rope(k_ref[0, rows, pl.ds(h * dk, dk)].astype(F32), cos, sin) * (dk ** -0.5)
            v = v_ref[0, rows, pl.ds(h * dv, dv)]
            scores = _mm_nt(q.astype(BF16), k.astype(BF16)) * decay_ref[h]
            inner.append(_mm(scores.astype(BF16), v))
            upd.append(_mm_tn((k * kdec_ref[h]).astype(BF16), v))
            qd.append((q * qdec_ref[h]).astype(BF16))

    for ci in range(chunks):
        rows = pl.ds(ci * c, c)
        for h in heads:
            p = ci * RET_HEADS + h
            state = state_ref[h]
            o = inner[p] + _mm(qd[p], state.astype(BF16))
            state_ref[h] = state * cdec_ref[h][0:1, 0:1] + upd[p]
            mu = jnp.mean(o, axis=-1, keepdims=True)
            var = jnp.mean(jnp.square(o - mu), axis=-1, keepdims=True)
            cols = pl.ds(h * dv, dv)
            on = (o - mu) * lax.rsqrt(var + NORM_EPS) * gain_ref[:, cols]
            y_ref[0, rows, cols] = (_silu(gate_ref[0, rows, cols].astype(F32)) * on).astype(y_ref.dtype)


def retention_core(proj, cos, sin, gn_gain, *, chunks=4):
    b, s, _ = proj.shape
    c = RET_CHUNK
    h = RET_HEADS
    rows = c * chunks
    log_gamma = jnp.log1p(-jnp.exp2(-5.0 - jnp.arange(h, dtype=F32)))
    pos = jnp.arange(c, dtype=F32)
    causal = pos[:, None] >= pos[None, :]
    diff = jnp.where(causal, pos[:, None] - pos[None, :], 0.0)
    decay = jnp.where(causal, jnp.exp(log_gamma[:, None, None] * diff), 0.0)
    qdec = jnp.exp(log_gamma[:, None] * (pos + 1.0))[:, :, None]
    kdec = jnp.exp(log_gamma[:, None] * (c - 1.0 - pos))[:, :, None]
    cdec = jnp.broadcast_to(jnp.exp(log_gamma * c)[:, None, None], (h, 8, 128))
    dk, dv = RET_HEAD_QK, RET_HEAD_V
    tile = lambda w, blk: pl.BlockSpec((1, rows, w), lambda bi, ni: (bi, ni, blk))
    whole = lambda *shape: pl.BlockSpec(shape, lambda bi, ni: (0,) * len(shape))
    return pl.pallas_call(
        functools.partial(_retention_kernel, chunks=chunks),
        grid=(b, s // rows),
        in_specs=[
            tile(RET_QK, 0), tile(RET_QK, 1), tile(RET_V, 1), tile(RET_V, 2),
            tile(dk // 2, 0), tile(dk // 2, 0),
            whole(h, c, c), whole(h, c, 1), whole(h, c, 1), whole(h, 8, 128), whole(1, RET_V),
        ],
        out_specs=tile(RET_V, 0),
        out_shape=jax.ShapeDtypeStruct((b, s, RET_V), BF16),
        scratch_shapes=[pltpu.VMEM((h, dk, dv), F32)],
        compiler_params=_params("parallel", "arbitrary"),
        name="retention_core",
    )(proj, proj, proj, proj, cos, sin, decay, qdec, kdec, cdec, gn_gain.reshape(1, RET_V))


def _gdn_pre_kernel(x_ref, prev_ref, ba_ref, cw_ref, alog_ref, dtb_ref,
                    q_ref, k_ref, v_ref, beta_ref, g_ref):
    first = pl.program_id(1) == 0
    t = x_ref.shape[1]
    row8 = lax.broadcasted_iota(jnp.int32, (8, GDN_HEAD), 0)

    def conv_silu(col):
        cols = pl.ds(col * GDN_HEAD, GDN_HEAD)
        x = x_ref[0, :, cols].astype(F32)
        prev = prev_ref[0, :, cols].astype(F32)[PREV_ROWS - 8:]
        prev = jnp.where(first, 0.0, prev)
        w = cw_ref[:, cols]
        acc = x * w[GDN_CONV_K - 1:GDN_CONV_K, :]
        for sh in range(1, GDN_CONV_K):
            xs = pltpu.roll(x, sh, 0)
            ps = pltpu.roll(prev, sh, 0)
            head = jnp.where(row8 < sh, ps, xs[:8])
            shifted = jnp.concatenate([head, xs[8:]], axis=0)
            acc = acc + shifted * w[GDN_CONV_K - 1 - sh:GDN_CONV_K - sh, :]
        return _silu(acc)

    def l2n(y):
        return y * lax.rsqrt(jnp.sum(y * y, axis=-1, keepdims=True) + NORM_EPS)

    for hh in range(GDN_K_HEADS):
        cols = pl.ds(hh * GDN_HEAD, GDN_HEAD)
        q_ref[0, :, cols] = l2n(conv_silu(hh)) * (GDN_HEAD ** -0.5)
        k_ref[0, :, cols] = l2n(conv_silu(GDN_K_HEADS + hh))
    for hh in range(GDN_V_HEADS):
        v_ref[0, :, pl.ds(hh * GDN_HEAD, GDN_HEAD)] = conv_silu(2 * GDN_K_HEADS + hh)

    ba = ba_ref[0]
    beta_ref[0] = _sigmoid(ba[:, :GDN_V_HEADS])
    a = ba[:, GDN_V_HEADS:] + dtb_ref[...]
    softplus = jnp.maximum(a, 0.0) + jnp.log1p(jnp.exp(-jnp.abs(a)))
    g_ref[0] = -jnp.exp(alog_ref[...]) * softplus


def gdn_pre(proj, ba, conv_w, a_log, dt_bias, *, ts=256):
    b, s, _ = proj.shape
    hv = GDN_V_HEADS
    f = lambda shape: jax.ShapeDtypeStruct(shape, F32)
    tile = lambda w: pl.BlockSpec((1, ts, w), lambda bi, si: (bi, si, 0))
    return pl.pallas_call(
        _gdn_pre_kernel, grid=(b, s // ts),
        in_specs=[
            tile(GDN_CONV_DIM),
            pl.BlockSpec((1, PREV_ROWS, GDN_CONV_DIM),
                         lambda bi, si: (bi, jnp.maximum(si * (ts // PREV_ROWS) - 1, 0), 0)),
            tile(2 * hv),
            pl.BlockSpec((GDN_CONV_K, GDN_CONV_DIM), lambda bi, si: (0, 0)),
            pl.BlockSpec((1, hv), lambda bi, si: (0, 0)),
            pl.BlockSpec((1, hv), lambda bi, si: (0, 0)),
        ],
        out_specs=[tile(GDN_QK), tile(GDN_QK), tile(GDN_V), tile(hv), tile(hv)],
        out_shape=[f((b, s, GDN_QK)), f((b, s, GDN_QK)), f((b, s, GDN_V)), f((b, s, hv)), f((b, s, hv))],
        compiler_params=_params("parallel", "parallel"),
        name="gdn_pre",
    )(proj, proj, ba, conv_w, a_log.reshape(1, hv), dt_bias.reshape(1, hv))


def _mm(a, b):
    return jnp.dot(a, b, preferred_element_type=F32)


def _mm_nt(a, b):
    return lax.dot_general(a, b, (((1,), (1,)), ((), ())), preferred_element_type=F32)


def _mm_tn(a, b):
    return lax.dot_general(a, b, (((0,), (0,)), ((), ())), preferred_element_type=F32)


def _bf(xs):
    return [x.astype(BF16) for x in xs]


def _unit_lower_inverses(a_list, eye, base_mask, level_masks):
    d = [jnp.where(base_mask, a, 0.0) for a in a_list]
    db = _bf(d)
    d2 = [_mm(x, x) for x in db]
    d2b = _bf(d2)
    p = [eye - x for x in d]
    p = [pi + _mm(pi.astype(BF16), x2) for pi, x2 in zip(p, d2b)]
    d4b = _bf([_mm(x, x) for x in d2b])
    t = [pi + _mm(pi.astype(BF16), x4) for pi, x4 in zip(p, d4b)]
    for mask in level_masks:
        eb = _bf([jnp.where(mask, a, 0.0) for a in a_list])
        tb = _bf(t)
        teb = _bf([_mm(ti, ei) for ti, ei in zip(tb, eb)])
        t = [ti - _mm(tei, tbi) for ti, tei, tbi in zip(t, teb, tb)]
    return t


def _gdn_core_kernel(q_ref, k_ref, v_ref, beta_ref, g_ref, z_ref, gain_ref, y_ref, state_ref, *, chunks, sub):
    c = GDN_CHUNK
    dh = GDN_HEAD
    nk = GDN_K_HEADS
    nv = GDN_V_HEADS
    rep = nv // nk

    @pl.when(pl.program_id(1) == 0)
    def _():
        state_ref[...] = jnp.zeros_like(state_ref)

    ri = lax.broadcasted_iota(jnp.int32, (c, c), 0)
    ci = lax.broadcasted_iota(jnp.int32, (c, c), 1)
    causal = ri >= ci
    strict = ri > ci
    eye = (ri == ci).astype(F32)
    tril = causal.astype(F32)
    same = lambda size: (ri // size) == (ci // size)
    base_mask = same(GDN_INV_BLOCK)
    level_masks = []
    size = GDN_INV_BLOCK
    while size < c:
        level_masks.append(same(2 * size) & jnp.logical_not(same(size)))
        size *= 2
    gain = gain_ref[...]

    def group_step(idx, carry):
        rows = [pl.ds(pl.multiple_of((idx * sub + s_) * c, c), c) for s_ in range(sub)]
        heads = range(nv)
        vprobs = [(s_, h) for s_ in range(sub) for h in heads]
        kprobs = [(s_, h) for s_ in range(sub) for h in range(nk)]
        col = lambda x, h: x[:, h:h + 1]

        gcum = [_hdot(tril, g_ref[0, r, :]) for r in rows]
        beta = [beta_ref[0, r, :] for r in rows]
        glast = [x[c - 1:c, :] for x in gcum]
        eg = [jnp.exp(x) for x in gcum]
        kdec = [jnp.exp(gl - x) for gl, x in zip(glast, gcum)]
        eglast = [jnp.exp(gl) for gl in glast]

        q = {(s_, h): q_ref[0, rows[s_], pl.ds(h * dh, dh)] for s_, h in kprobs}
        k = {(s_, h): k_ref[0, rows[s_], pl.ds(h * dh, dh)] for s_, h in kprobs}
        k16 = {p: k[p].astype(BF16) for p in kprobs}
        kk = {p: _mm_nt(k16[p], k16[p]) for p in kprobs}
        qk = {p: _mm_nt(q[p].astype(BF16), k16[p]) for p in kprobs}

        decay, a = {}, []
        for s_, h in vprobs:
            gcol = col(gcum[s_], h)
            grow = jnp.sum(gcol * eye, axis=0, keepdims=True)
            decay[s_, h] = jnp.where(causal, jnp.exp(jnp.where(causal, gcol - grow, 0.0)), 0.0)
            a.append(jnp.where(strict, kk[s_, h // rep] * col(beta[s_], h) * decay[s_, h], 0.0))
        t = _unit_lower_inverses(a, eye, base_mask, level_masks)

        rhs = []
        for s_, h in vprobs:
            b_col = col(beta[s_], h)
            v = v_ref[0, rows[s_], pl.ds(h * dh, dh)]
            rhs.append(jnp.concatenate([v * b_col, k[s_, h // rep] * (b_col * col(eg[s_], h))],
                                       axis=-1).astype(BF16))
        sol = dict(zip(vprobs, [_mm(ti, ri_) for ti, ri_ in zip(_bf(t), rhs)]))
        attn = {(s_, h): jnp.where(causal, qk[s_, h // rep] * decay[s_, h], 0.0).astype(BF16)
                for s_, h in vprobs}
        wq = {(s_, h): jnp.concatenate([sol[s_, h][:, dh:], q[s_, h // rep] * col(eg[s_], h)],
                                       axis=0).astype(BF16) for s_, h in vprobs}
        kd = {(s_, h): (k[s_, h // rep] * col(kdec[s_], h)).astype(BF16) for s_, h in vprobs}

        state = [state_ref[h] for h in heads]
        for s_ in range(sub):
            ws_qs = [_mm(wq[s_, h], state[h].astype(BF16)) for h in heads]
            v_new = _bf([sol[s_, h][:, :dh] - ws_qs[h][:c] for h in heads])
            o = [ws_qs[h][c:] + _mm(attn[s_, h], v_new[h]) for h in heads]
            state = [state[h] * eglast[s_][:, h:h + 1] + _mm_tn(kd[s_, h], v_new[h]) for h in heads]
            for h in heads:
                on = o[h] * lax.rsqrt(jnp.mean(o[h] * o[h], axis=-1, keepdims=True) + NORM_EPS) * gain
                cols = pl.ds(h * dh, dh)
                y_ref[0, rows[s_], cols] = (on * _silu(z_ref[0, rows[s_], cols].astype(F32))).astype(y_ref.dtype)
        for h in heads:
            state_ref[h] = state[h]
        return carry

    lax.fori_loop(0, chunks // sub, group_step, 0)


def gdn_core(q, k, v, beta, g, proj, norm_gain, *, chunks=4, sub=2):
    b, s, _ = q.shape
    rows = GDN_CHUNK * chunks
    hv = GDN_V_HEADS
    tile = lambda w, blk=0: pl.BlockSpec((1, rows, w), lambda bi, ni: (bi, ni, blk))
    return pl.pallas_call(
        functools.partial(_gdn_core_kernel, chunks=chunks, sub=sub), grid=(b, s // rows),
        in_specs=[tile(GDN_QK), tile(GDN_QK), tile(GDN_V), tile(hv), tile(hv),
                  tile(GDN_V, GDN_CONV_DIM // GDN_V),
                  pl.BlockSpec((1, GDN_HEAD), lambda bi, ni: (0, 0))],
        out_specs=tile(GDN_V),
        out_shape=jax.ShapeDtypeStruct((b, s, GDN_V), BF16),
        scratch_shapes=[pltpu.VMEM((hv, GDN_HEAD, GDN_HEAD), F32)],
        compiler_params=_params("parallel", "arbitrary"),
        name="gdn_core",
    )(q, k, v, beta, g, proj, norm_gain.reshape(1, GDN_HEAD))


def _out_proj_kernel(h_ref, y_ref, w_ref, o_ref):
    o_ref[...] = h_ref[...] + jnp.dot(y_ref[...], w_ref[...], preferred_element_type=F32)


def out_proj_residual(h, y, w, *, tm=512):
    m, d = h.shape
    kdim = y.shape[1]
    return pl.pallas_call(
        _out_proj_kernel, grid=(m // tm,),
        in_specs=[pl.BlockSpec((tm, d), lambda i: (i, 0)),
                  pl.BlockSpec((tm, kdim), lambda i: (i, 0)),
                  pl.BlockSpec((kdim, d), lambda i: (0, 0))],
        out_specs=pl.BlockSpec((tm, d), lambda i: (i, 0)),
        out_shape=jax.ShapeDtypeStruct((m, d), F32),
        compiler_params=_params("parallel"),
        name="out_proj_residual",
    )(h, y, w)


def _ffn_kernel(h_ref, g_ref, win_ref, wout_ref, fg_ref, o_ref, *, hidden_chunk, final_norm):
    x = h_ref[...]
    xn = _rmsnorm_rows(x, g_ref[...]).astype(BF16)
    acc = x
    for c0 in range(0, FFN_HIDDEN, hidden_chunk):
        gate = jnp.dot(xn, win_ref[:, pl.ds(c0, hidden_chunk)], preferred_element_type=F32)
        up = jnp.dot(xn, win_ref[:, pl.ds(FFN_HIDDEN + c0, hidden_chunk)], preferred_element_type=F32)
        act = (_silu(gate) * up).astype(BF16)
        acc = acc + jnp.dot(act, wout_ref[pl.ds(c0, hidden_chunk), :], preferred_element_type=F32)
    if final_norm:
        acc = _rmsnorm_rows(acc, fg_ref[...])
    o_ref[...] = acc


def ffn_residual(h, gain, w_in, w_out, final_gain, *, final_norm, tm=512, hidden_chunk=1408):
    m, d = h.shape
    resident = functools.partial(pl.BlockSpec, index_map=lambda i: (0, 0), pipeline_mode=pl.Buffered(1))
    return pl.pallas_call(
        functools.partial(_ffn_kernel, hidden_chunk=hidden_chunk, final_norm=final_norm),
        grid=(m // tm,),
        in_specs=[pl.BlockSpec((tm, d), lambda i: (i, 0)),
                  pl.BlockSpec((1, d), lambda i: (0, 0)),
                  resident((d, 2 * FFN_HIDDEN)),
                  resident((FFN_HIDDEN, d)),
                  pl.BlockSpec((1, d), lambda i: (0, 0))],
        out_specs=pl.BlockSpec((tm, d), lambda i: (i, 0)),
        out_shape=jax.ShapeDtypeStruct((m, d), F32),
        compiler_params=_params("parallel"),
        name="ffn_residual",
    )(h, gain, w_in, w_out, final_gain)


def kernel(x, positions, norm_mix, norm_ffn, norm_final, ret_w_in, ret_gn_gain, ret_w_out,
           gdn_w_in, gdn_conv, gdn_a_log, gdn_dt_bias, gdn_norm_gain, gdn_w_out,
           ffn_w_in, ffn_w_out):
    b, s, d = x.shape
    m = b * s
    h = x.reshape(m, d)
    cos, sin = rope_tables(positions)
    final_gain = norm_final.reshape(1, d)
    for i in range(DEPTH):
        j = i // N_MIXERS
        gain = norm_mix[i].reshape(1, d)
        if i % N_MIXERS == 0:
            proj = norm_proj(h, gain, ret_w_in[j].astype(BF16))
            y = retention_core(proj.reshape(b, s, RET_IN), cos, sin, ret_gn_gain[j])
            w_out = ret_w_out[j]
        else:
            w_in = gdn_w_in[j]
            proj, ba = norm_proj(h, gain, w_in[:, :GDN_MAIN].astype(BF16), w_in[:, GDN_MAIN:].astype(BF16))
            proj = proj.reshape(b, s, GDN_MAIN)
            q, k, v, beta, g = gdn_pre(proj, ba.reshape(b, s, -1), gdn_conv[j], gdn_a_log[j], gdn_dt_bias[j])
            y = gdn_core(q, k, v, beta, g, proj, gdn_norm_gain[j])
            w_out = gdn_w_out[j]
        h = out_proj_residual(h, y.reshape(m, -1), w_out.astype(BF16))
        h = ffn_residual(h, norm_ffn[i].reshape(1, d), ffn_w_in[i].astype(BF16), ffn_w_out[i].astype(BF16),
                         final_gain, final_norm=(i == DEPTH - 1))
    return h.reshape(b, s, d)
```

```python
import functools

import jax
import jax.numpy as jnp
from jax import lax
from jax.experimental import pallas as pl
from jax.experimental.pallas import tpu as pltpu

F32 = jnp.float32
BF16 = jnp.bfloat16
HIGHEST = lax.Precision.HIGHEST

D_MODEL = 1024
DEPTH = 4
N_MIXERS = 2

RET_HEADS = 4
RET_HEAD_QK = D_MODEL // RET_HEADS
RET_HEAD_V = 2 * RET_HEAD_QK
RET_QK = RET_HEADS * RET_HEAD_QK
RET_V = RET_HEADS * RET_HEAD_V
RET_IN = 2 * RET_QK + 2 * RET_V
RET_CHUNK = 128
ROPE_THETA = 10000.0

GDN_HEAD = 128
GDN_K_HEADS = D_MODEL // GDN_HEAD
GDN_V_HEADS = 2 * GDN_K_HEADS
GDN_QK = GDN_K_HEADS * GDN_HEAD
GDN_V = GDN_V_HEADS * GDN_HEAD
GDN_CONV_DIM = 2 * GDN_QK + GDN_V
GDN_MAIN = GDN_CONV_DIM + GDN_V
GDN_CONV_K = 4
GDN_CHUNK = 64
GDN_INV_BLOCK = 8
PREV_ROWS = 16
GDN_PROJ_GROUP = 256

FFN_HIDDEN = 2816
NORM_EPS = 1e-6

VMEM_LIMIT = 48 * 1024 * 1024


def _params(*sem):
    return pltpu.CompilerParams(dimension_semantics=sem, vmem_limit_bytes=VMEM_LIMIT)


def _bdot(a, b):
    return jnp.dot(a.astype(BF16), b.astype(BF16), preferred_element_type=F32)


def _bdot_nt(a, b):
    return lax.dot_general(a.astype(BF16), b.astype(BF16), (((1,), (1,)), ((), ())),
                           preferred_element_type=F32)


def _bdot_tn(a, b):
    return lax.dot_general(a.astype(BF16), b.astype(BF16), (((0,), (0,)), ((), ())),
                           preferred_element_type=F32)


def _hdot(a, b):
    return jnp.dot(a, b, precision=HIGHEST, preferred_element_type=F32)


def _sigmoid(x):
    return 1.0 / (1.0 + jnp.exp(-x))


def _silu(x):
    half = 0.5 * x
    return half + half * jnp.tanh(half)


def _rmsnorm_rows(x, gain):
    return x * lax.rsqrt(jnp.mean(x * x, axis=-1, keepdims=True) + NORM_EPS) * gain


def _norm_proj_kernel(x_ref, g_ref, w_ref, o_ref, xn_ref):
    @pl.when(pl.program_id(1) == 0)
    def _():
        xn_ref[...] = _rmsnorm_rows(x_ref[...], g_ref[...]).astype(BF16)

    o_ref[...] = jnp.dot(xn_ref[...], w_ref[...], preferred_element_type=F32).astype(o_ref.dtype)


def norm_proj(x, gain, w, *, tm=1024, tn=1536):
    m, d = x.shape
    n = w.shape[1]
    return pl.pallas_call(
        _norm_proj_kernel, grid=(m // tm, n // tn),
        in_specs=[pl.BlockSpec((tm, d), lambda i, j: (i, 0)),
                  pl.BlockSpec((1, d), lambda i, j: (0, 0)),
                  pl.BlockSpec((d, tn), lambda i, j: (0, j))],
        out_specs=pl.BlockSpec((tm, tn), lambda i, j: (i, j)),
        out_shape=jax.ShapeDtypeStruct((m, n), BF16),
        scratch_shapes=[pltpu.VMEM((tm, d), BF16)],
        compiler_params=_params("parallel", "arbitrary"),
        name="norm_proj",
    )(x, gain, w)


def _rope_table_kernel(pos_ref, freq_ref, cos_ref, sin_ref):
    ang = pos_ref[0].astype(F32) * freq_ref[...]
    cos_ref[0] = jnp.cos(ang)
    sin_ref[0] = jnp.sin(ang)


def rope_tables(positions, *, ts=512):
    b, s = positions.shape
    half = RET_HEAD_QK // 2
    inv_freq = (ROPE_THETA ** (-jnp.arange(half, dtype=F32) / half)).reshape(1, half)
    out = jax.ShapeDtypeStruct((b, s, half), F32)
    return pl.pallas_call(
        _rope_table_kernel, grid=(b, s // ts),
        in_specs=[pl.BlockSpec((1, ts, 1), lambda i, j: (i, j, 0)),
                  pl.BlockSpec((1, half), lambda i, j: (0, 0))],
        out_specs=[pl.BlockSpec((1, ts, half), lambda i, j: (i, j, 0))] * 2,
        out_shape=[out, out], compiler_params=_params("parallel", "parallel"),
        name="rope_tables",
    )(positions.reshape(b, s, 1), inv_freq)


def _retention_kernel(q_ref, k_ref, v_ref, gate_ref, cos_ref, sin_ref, decay_ref, qdec_ref,
                      kdec_ref, cdec_ref, gain_ref, y_ref, state_ref, *, chunks):
    c = RET_CHUNK
    dk, dv = RET_HEAD_QK, RET_HEAD_V
    half = dk // 2
    heads = range(RET_HEADS)

    @pl.when(pl.program_id(1) == 0)
    def _():
        state_ref[...] = jnp.zeros_like(state_ref)

    def rope(t, cos, sin):
        t1, t2 = t[:, :half], t[:, half:]
        return jnp.concatenate([t1 * cos - t2 * sin, t2 * cos + t1 * sin], axis=-1)

    qd, inner, upd = [], [], []
    for ci in range(chunks):
        rows = pl.ds(ci * c, c)
        cos = cos_ref[0, rows, :]
        sin = sin_ref[0, rows, :]
        for h in heads:
            q = rope(q_ref[0, rows, pl.ds(h * dk, dk)].astype(F32), cos, sin)
            k = rope(k_ref[0, rows, pl.ds(h * dk, dk)].astype(F32), cos, sin) * (dk ** -0.5)
            v = v_ref[0, rows, pl.ds(h * dv, dv)]
            scores = _mm_nt(q.astype(BF16), k.astype(BF16)) * decay_ref[h]
            inner.append(_mm(scores.astype(BF16), v))
            upd.append(_mm_tn((k * kdec_ref[h]).astype(BF16), v))
            qd.append((q * qdec_ref[h]).astype(BF16))

    for ci in range(chunks):
        rows = pl.ds(ci * c, c)
        for h in heads:
            p = ci * RET_HEADS + h
            state = state_ref[h]
            o = inner[p] + _mm(qd[p], state.astype(BF16))
            state_ref[h] = state * cdec_ref[h][0:1, 0:1] + upd[p]
            mu = jnp.mean(o, axis=-1, keepdims=True)
            var = jnp.mean(jnp.square(o - mu), axis=-1, keepdims=True)
            cols = pl.ds(h * dv, dv)
            on = (o - mu) * lax.rsqrt(var + NORM_EPS) * gain_ref[:, cols]
            y_ref[0, rows, cols] = (_silu(gate_ref[0, rows, cols].astype(F32)) * on).astype(y_ref.dtype)


def retention_core(proj, cos, sin, gn_gain, *, chunks=4):
    b, s, _ = proj.shape
    c = RET_CHUNK
    h = RET_HEADS
    rows = c * chunks
    log_gamma = jnp.log1p(-jnp.exp2(-5.0 - jnp.arange(h, dtype=F32)))
    pos = jnp.arange(c, dtype=F32)
    causal = pos[:, None] >= pos[None, :]
    diff = jnp.where(causal, pos[:, None] - pos[None, :], 0.0)
    decay = jnp.where(causal, jnp.exp(log_gamma[:, None, None] * diff), 0.0)
    qdec = jnp.exp(log_gamma[:, None] * (pos + 1.0))[:, :, None]
    kdec = jnp.exp(log_gamma[:, None] * (c - 1.0 - pos))[:, :, None]
    cdec = jnp.broadcast_to(jnp.exp(log_gamma * c)[:, None, None], (h, 8, 128))
    dk, dv = RET_HEAD_QK, RET_HEAD_V
    tile = lambda w, blk: pl.BlockSpec((1, rows, w), lambda bi, ni: (bi, ni, blk))
    whole = lambda *shape: pl.BlockSpec(shape, lambda bi, ni: (0,) * len(shape))
    return pl.pallas_call(
        functools.partial(_retention_kernel, chunks=chunks),
        grid=(b, s // rows),
        in_specs=[
            tile(RET_QK, 0), tile(RET_QK, 1), tile(RET_V, 1), tile(RET_V, 2),
            tile(dk // 2, 0), tile(dk // 2, 0),
            whole(h, c, c), whole(h, c, 1), whole(h, c, 1), whole(h, 8, 128), whole(1, RET_V),
        ],
        out_specs=tile(RET_V, 0),
        out_shape=jax.ShapeDtypeStruct((b, s, RET_V), BF16),
        scratch_shapes=[pltpu.VMEM((h, dk, dv), F32)],
        compiler_params=_params("parallel", "arbitrary"),
        name="retention_core",
    )(proj, proj, proj, proj, cos, sin, decay, qdec, kdec, cdec, gn_gain.reshape(1, RET_V))


def _gdn_proj_kernel(x_ref, xp_ref, g_ref, w_ref, ws_ref, cw_ref, alog_ref, dtb_ref,
                     o_ref, beta_ref, gl_ref, xn_ref, ybuf_ref, *, tiles_per_seq):
    i = pl.program_id(0)
    j = pl.program_id(1)
    p = PREV_ROWS
    gw = GDN_PROJ_GROUP
    hv = GDN_V_HEADS

    @pl.when(j == 0)
    def _():
        first = (i % tiles_per_seq) == 0
        xp = _rmsnorm_rows(xp_ref[...], g_ref[...])
        xn_ref[0:p, :] = jnp.where(first, 0.0, xp).astype(BF16)
        xn = _rmsnorm_rows(x_ref[...], g_ref[...]).astype(BF16)
        xn_ref[p:, :] = xn
        ba = jnp.dot(xn, ws_ref[...], preferred_element_type=F32)
        beta_ref[...] = _sigmoid(ba[:, :hv])
        a = ba[:, hv:] + dtb_ref[...]
        softplus = jnp.maximum(a, 0.0) + jnp.log1p(jnp.exp(-jnp.abs(a)))
        gl_ref[...] = -jnp.exp(alog_ref[...]) * softplus

    groups = o_ref.shape[1] // gw
    tm = x_ref.shape[0]

    def project(group):
        return jnp.dot(xn_ref[...], w_ref[:, pl.ds(group * gw, gw)], preferred_element_type=F32)

    def conv_silu(y, group):
        cw = cw_ref[:, pl.ds(group * gw, gw)]
        ybuf = ybuf_ref.at[group % 2]
        ybuf[...] = y
        acc = y[p:] * cw[GDN_CONV_K - 1:GDN_CONV_K, :]
        for sh in range(1, GDN_CONV_K):
            acc = acc + ybuf[pl.ds(p - sh, tm), :] * cw[GDN_CONV_K - 1 - sh:GDN_CONV_K - sh, :]
        return _silu(acc)

    def l2n(y, scale):
        return y * (lax.rsqrt(jnp.sum(y * y, axis=-1, keepdims=True) + NORM_EPS) * scale)

    def conv_groups(finish):
        y_next = project(0)
        for group in range(groups):
            y = y_next
            if group + 1 < groups:
                y_next = project(group + 1)
            o_ref[:, pl.ds(group * gw, gw)] = finish(conv_silu(y, group), group).astype(o_ref.dtype)

    def qk_finish(s, group):
        scale = GDN_HEAD ** -0.5 if group * gw < GDN_QK else 1.0
        return jnp.concatenate([l2n(s[:, c0:c0 + GDN_HEAD], scale) for c0 in range(0, gw, GDN_HEAD)], axis=-1)

    @pl.when(j == 0)
    def _():
        conv_groups(qk_finish)

    @pl.when(j == 1)
    def _():
        conv_groups(lambda s, group: s)

    @pl.when(j == 2)
    def _():
        o_ref[...] = jnp.dot(xn_ref[p:, :], w_ref[...], preferred_element_type=F32).astype(o_ref.dtype)


def gdn_proj(x, gain, w_main, w_small, conv_w, a_log, dt_bias, *, seq, tm=1024):
    m, d = x.shape
    hv = GDN_V_HEADS
    tn = GDN_V
    assert GDN_CONV_DIM == 2 * tn and 2 * GDN_QK == tn and seq % tm == 0
    small = lambda shape: pl.BlockSpec(shape, lambda i, j: (0, 0))
    return pl.pallas_call(
        functools.partial(_gdn_proj_kernel, tiles_per_seq=seq // tm),
        grid=(m // tm, GDN_MAIN // tn),
        in_specs=[
            pl.BlockSpec((tm, d), lambda i, j: (i, 0)),
            pl.BlockSpec((PREV_ROWS, d), lambda i, j: (jnp.maximum(i * (tm // PREV_ROWS) - 1, 0), 0)),
            small((1, d)),
            pl.BlockSpec((d, tn), lambda i, j: (0, j)),
            small((d, 2 * hv)),
            pl.BlockSpec((GDN_CONV_K, tn), lambda i, j: (0, jnp.minimum(j, 1))),
            small((1, hv)), small((1, hv)),
        ],
        out_specs=[pl.BlockSpec((tm, tn), lambda i, j: (i, j)),
                   pl.BlockSpec((tm, hv), lambda i, j: (i, 0)),
                   pl.BlockSpec((tm, hv), lambda i, j: (i, 0))],
        out_shape=[jax.ShapeDtypeStruct((m, GDN_MAIN), BF16),
                   jax.ShapeDtypeStruct((m, hv), F32), jax.ShapeDtypeStruct((m, hv), F32)],
        scratch_shapes=[pltpu.VMEM((PREV_ROWS + tm, d), BF16),
                        pltpu.VMEM((2, PREV_ROWS + tm, GDN_PROJ_GROUP), F32)],
        compiler_params=_params("parallel", "arbitrary"),
        name="gdn_proj",
    )(x, x, gain, w_main, w_small, conv_w, a_log.reshape(1, hv), dt_bias.reshape(1, hv))


def _mm(a, b):
    return jnp.dot(a, b, preferred_element_type=F32)


def _mm_nt(a, b):
    return lax.dot_general(a, b, (((1,), (1,)), ((), ())), preferred_element_type=F32)


def _mm_tn(a, b):
    return lax.dot_general(a, b, (((0,), (0,)), ((), ())), preferred_element_type=F32)


def _bf(xs):
    return [x.astype(BF16) for x in xs]


def _unit_lower_inverses(a_list, eye, base_mask, level_masks):
    d = [jnp.where(base_mask, a, 0.0) for a in a_list]
    db = _bf(d)
    d2 = [_mm(x, x) for x in db]
    d2b = _bf(d2)
    p = [eye - x for x in d]
    p = [pi + _mm(pi.astype(BF16), x2) for pi, x2 in zip(p, d2b)]
    d4b = _bf([_mm(x, x) for x in d2b])
    t = [pi + _mm(pi.astype(BF16), x4) for pi, x4 in zip(p, d4b)]
    for mask in level_masks:
        eb = _bf([jnp.where(mask, a, 0.0) for a in a_list])
        tb = _bf(t)
        teb = _bf([_mm(ti, ei) for ti, ei in zip(tb, eb)])
        t = [ti - _mm(tei, tbi) for ti, tei, tbi in zip(t, teb, tb)]
    return t


def _gdn_core_kernel(q_ref, k_ref, v_ref, beta_ref, g_ref, z_ref, gain_ref, y_ref, state_ref, *, chunks, sub):
    c = GDN_CHUNK
    dh = GDN_HEAD
    nk = GDN_K_HEADS
    nv = GDN_V_HEADS
    rep = nv // nk

    @pl.when(pl.program_id(1) == 0)
    def _():
        state_ref[...] = jnp.zeros_like(state_ref)

    ri = lax.broadcasted_iota(jnp.int32, (c, c), 0)
    ci = lax.broadcasted_iota(jnp.int32, (c, c), 1)
    causal = ri >= ci
    strict = ri > ci
    eye = (ri == ci).astype(F32)
    tril = causal.astype(F32)
    same = lambda size: (ri // size) == (ci // size)
    base_mask = same(GDN_INV_BLOCK)
    level_masks = []
    size = GDN_INV_BLOCK
    while size < c:
        level_masks.append(same(2 * size) & jnp.logical_not(same(size)))
        size *= 2
    gain = gain_ref[...]

    def group_step(idx, carry):
        rows = [pl.ds(pl.multiple_of((idx * sub + s_) * c, c), c) for s_ in range(sub)]
        heads = range(nv)
        vprobs = [(s_, h) for s_ in range(sub) for h in heads]
        kprobs = [(s_, h) for s_ in range(sub) for h in range(nk)]
        col = lambda x, h: x[:, h:h + 1]

        gcum = [_hdot(tril, g_ref[0, r, :]) for r in rows]
        beta = [beta_ref[0, r, :] for r in rows]
        glast = [x[c - 1:c, :] for x in gcum]
        eg = [jnp.exp(x) for x in gcum]
        kdec = [jnp.exp(gl - x) for gl, x in zip(glast, gcum)]
        eglast = [jnp.exp(gl) for gl in glast]

        q16 = {(s_, h): q_ref[0, rows[s_], pl.ds(h * dh, dh)] for s_, h in kprobs}
        k16 = {(s_, h): k_ref[0, rows[s_], pl.ds(h * dh, dh)] for s_, h in kprobs}
        q = {p: q16[p].astype(F32) for p in kprobs}
        k = {p: k16[p].astype(F32) for p in kprobs}
        kk = {p: _mm_nt(k16[p], k16[p]) for p in kprobs}
        qk = {p: _mm_nt(q16[p], k16[p]) for p in kprobs}

        decay, a = {}, []
        for s_, h in vprobs:
            gcol = col(gcum[s_], h)
            grow = jnp.sum(gcol * eye, axis=0, keepdims=True)
            decay[s_, h] = jnp.where(causal, jnp.exp(jnp.where(causal, gcol - grow, 0.0)), 0.0)
            a.append(jnp.where(strict, kk[s_, h // rep] * col(beta[s_], h) * decay[s_, h], 0.0))
        t = _unit_lower_inverses(a, eye, base_mask, level_masks)

        rhs = []
        for s_, h in vprobs:
            b_col = col(beta[s_], h)
            v = v_ref[0, rows[s_], pl.ds(h * dh, dh)].astype(F32)
            rhs.append(jnp.concatenate([v * b_col, k[s_, h // rep] * (b_col * col(eg[s_], h))],
                                       axis=-1).astype(BF16))
        sol = dict(zip(vprobs, [_mm(ti, ri_) for ti, ri_ in zip(_bf(t), rhs)]))
        attn = {(s_, h): jnp.where(causal, qk[s_, h // rep] * decay[s_, h], 0.0).astype(BF16)
                for s_, h in vprobs}
        wq = {(s_, h): jnp.concatenate([sol[s_, h][:, dh:], q[s_, h // rep] * col(eg[s_], h)],
                                       axis=0).astype(BF16) for s_, h in vprobs}
        kd = {(s_, h): (k[s_, h // rep] * col(kdec[s_], h)).astype(BF16) for s_, h in vprobs}

        state = [state_ref[h] for h in heads]
        for s_ in range(sub):
            ws_qs = [_mm(wq[s_, h], state[h].astype(BF16)) for h in heads]
            v_new = _bf([sol[s_, h][:, :dh] - ws_qs[h][:c] for h in heads])
            o = [ws_qs[h][c:] + _mm(attn[s_, h], v_new[h]) for h in heads]
            state = [state[h] * eglast[s_][:, h:h + 1] + _mm_tn(kd[s_, h], v_new[h]) for h in heads]
            for h in heads:
                on = o[h] * lax.rsqrt(jnp.mean(o[h] * o[h], axis=-1, keepdims=True) + NORM_EPS) * gain
                cols = pl.ds(h * dh, dh)
                y_ref[0, rows[s_], cols] = (on * _silu(z_ref[0, rows[s_], cols].astype(F32))).astype(y_ref.dtype)
        for h in heads:
            state_ref[h] = state[h]
        return carry

    lax.fori_loop(0, chunks // sub, group_step, 0)


def gdn_core(proj, beta, g, norm_gain, *, chunks=4, sub=2):
    b, s, _ = proj.shape
    rows = GDN_CHUNK * chunks
    hv = GDN_V_HEADS
    tile = lambda w, blk=0: pl.BlockSpec((1, rows, w), lambda bi, ni: (bi, ni, blk))
    return pl.pallas_call(
        functools.partial(_gdn_core_kernel, chunks=chunks, sub=sub), grid=(b, s // rows),
        in_specs=[tile(GDN_QK, 0), tile(GDN_QK, 1), tile(GDN_V, 1), tile(hv), tile(hv),
                  tile(GDN_V, 2),
                  pl.BlockSpec((1, GDN_HEAD), lambda bi, ni: (0, 0))],
        out_specs=tile(GDN_V),
        out_shape=jax.ShapeDtypeStruct((b, s, GDN_V), BF16),
        scratch_shapes=[pltpu.VMEM((hv, GDN_HEAD, GDN_HEAD), F32)],
        compiler_params=_params("parallel", "arbitrary"),
        name="gdn_core",
    )(proj, proj, proj, beta, g, proj, norm_gain.reshape(1, GDN_HEAD))


def _out_proj_kernel(h_ref, y_ref, w_ref, o_ref):
    o_ref[...] = h_ref[...] + jnp.dot(y_ref[...], w_ref[...], preferred_element_type=F32)


def out_proj_residual(h, y, w, *, tm=512):
    m, d = h.shape
    kdim = y.shape[1]
    return pl.pallas_call(
        _out_proj_kernel, grid=(m // tm,),
        in_specs=[pl.BlockSpec((tm, d), lambda i: (i, 0)),
                  pl.BlockSpec((tm, kdim), lambda i: (i, 0)),
                  pl.BlockSpec((kdim, d), lambda i: (0, 0))],
        out_specs=pl.BlockSpec((tm, d), lambda i: (i, 0)),
        out_shape=jax.ShapeDtypeStruct((m, d), F32),
        compiler_params=_params("parallel"),
        name="out_proj_residual",
    )(h, y, w)


def _ffn_kernel(h_ref, g_ref, win_ref, wout_ref, fg_ref, o_ref, *, hidden_chunk, final_norm):
    x = h_ref[...]
    xn = _rmsnorm_rows(x, g_ref[...]).astype(BF16)
    acc = x
    for c0 in range(0, FFN_HIDDEN, hidden_chunk):
        gate = jnp.dot(xn, win_ref[:, pl.ds(c0, hidden_chunk)], preferred_element_type=F32)
        up = jnp.dot(xn, win_ref[:, pl.ds(FFN_HIDDEN + c0, hidden_chunk)], preferred_element_type=F32)
        act = (_silu(gate) * up).astype(BF16)
        acc = acc + jnp.dot(act, wout_ref[pl.ds(c0, hidden_chunk), :], preferred_element_type=F32)
    if final_norm:
        acc = _rmsnorm_rows(acc, fg_ref[...])
    o_ref[...] = acc


def ffn_residual(h, gain, w_in, w_out, final_gain, *, final_norm, tm=512, hidden_chunk=1408):
    m, d = h.shape
    resident = functools.partial(pl.BlockSpec, index_map=lambda i: (0, 0), pipeline_mode=pl.Buffered(1))
    return pl.pallas_call(
        functools.partial(_ffn_kernel, hidden_chunk=hidden_chunk, final_norm=final_norm),
        grid=(m // tm,),
        in_specs=[pl.BlockSpec((tm, d), lambda i: (i, 0)),
                  pl.BlockSpec((1, d), lambda i: (0, 0)),
                  resident((d, 2 * FFN_HIDDEN)),
                  resident((FFN_HIDDEN, d)),
                  pl.BlockSpec((1, d), lambda i: (0, 0))],
        out_specs=pl.BlockSpec((tm, d), lambda i: (i, 0)),
        out_shape=jax.ShapeDtypeStruct((m, d), F32),
        compiler_params=_params("parallel"),
        name="ffn_residual",
    )(h, gain, w_in, w_out, final_gain)


def kernel(x, positions, norm_mix, norm_ffn, norm_final, ret_w_in, ret_gn_gain, ret_w_out,
           gdn_w_in, gdn_conv, gdn_a_log, gdn_dt_bias, gdn_norm_gain, gdn_w_out,
           ffn_w_in, ffn_w_out):
    b, s, d = x.shape
    m = b * s
    h = x.reshape(m, d)
    cos, sin = rope_tables(positions)
    final_gain = norm_final.reshape(1, d)
    for i in range(DEPTH):
        j = i // N_MIXERS
        gain = norm_mix[i].reshape(1, d)
        if i % N_MIXERS == 0:
            proj = norm_proj(h, gain, ret_w_in[j].astype(BF16))
            y = retention_core(proj.reshape(b, s, RET_IN), cos, sin, ret_gn_gain[j])
            w_out = ret_w_out[j]
        else:
            w_in = gdn_w_in[j]
            proj, beta, g = gdn_proj(h, gain, w_in[:, :GDN_MAIN].astype(BF16), w_in[:, GDN_MAIN:].astype(BF16),
                                     gdn_conv[j], gdn_a_log[j], gdn_dt_bias[j], seq=s)
            hv = GDN_V_HEADS
            y = gdn_core(proj.reshape(b, s, GDN_MAIN), beta.reshape(b, s, hv), g.reshape(b, s, hv),
                         gdn_norm_gain[j])
            w_out = gdn_w_out[j]
        h = out_proj_residual(h, y.reshape(m, -1), w_out.astype(BF16))
        h = ffn_residual(h, norm_ffn[i].reshape(1, d), ffn_w_in[i].astype(BF16), ffn_w_out[i].astype(BF16),
                         final_gain, final_norm=(i == DEPTH - 1))
    return h.reshape(b, s, d)
```

```python
import functools

import jax
import jax.numpy as jnp
from jax import lax
from jax.experimental import pallas as pl
from jax.experimental.pallas import tpu as pltpu

F32 = jnp.float32
BF16 = jnp.bfloat16
HIGHEST = lax.Precision.HIGHEST

D_MODEL = 1024
DEPTH = 4
N_MIXERS = 2

RET_HEADS = 4
RET_HEAD_QK = D_MODEL // RET_HEADS
RET_HEAD_V = 2 * RET_HEAD_QK
RET_QK = RET_HEADS * RET_HEAD_QK
RET_V = RET_HEADS * RET_HEAD_V
RET_IN = 2 * RET_QK + 2 * RET_V
RET_CHUNK = 128
ROPE_THETA = 10000.0

GDN_HEAD = 128
GDN_K_HEADS = D_MODEL // GDN_HEAD
GDN_V_HEADS = 2 * GDN_K_HEADS
GDN_QK = GDN_K_HEADS * GDN_HEAD
GDN_V = GDN_V_HEADS * GDN_HEAD
GDN_CONV_DIM = 2 * GDN_QK + GDN_V
GDN_MAIN = GDN_CONV_DIM + GDN_V
GDN_CONV_K = 4
GDN_CHUNK = 64
GDN_INV_BLOCK = 8
PREV_ROWS = 16
GDN_PROJ_GROUP = 256

FFN_HIDDEN = 2816
NORM_EPS = 1e-6

VMEM_LIMIT = 48 * 1024 * 1024


def _params(*sem):
    return pltpu.CompilerParams(dimension_semantics=sem, vmem_limit_bytes=VMEM_LIMIT)


def _mm(a, b):
    return jnp.dot(a, b, preferred_element_type=F32)


def _mm_nt(a, b):
    return lax.dot_general(a, b, (((1,), (1,)), ((), ())), preferred_element_type=F32)


def _mm_tn(a, b):
    return lax.dot_general(a, b, (((0,), (0,)), ((), ())), preferred_element_type=F32)


def _hdot(a, b):
    return jnp.dot(a, b, precision=HIGHEST, preferred_element_type=F32)


def _bf(xs):
    return [x.astype(BF16) for x in xs]


def _sigmoid(x):
    return 1.0 / (1.0 + jnp.exp(-x))


def _silu(x):
    half = 0.5 * x
    return half + half * jnp.tanh(half)


def _rmsnorm_rows(x, gain):
    return x * lax.rsqrt(jnp.mean(x * x, axis=-1, keepdims=True) + NORM_EPS) * gain


def _norm_proj_kernel(x_ref, g_ref, w_ref, o_ref, xn_ref):
    @pl.when(pl.program_id(1) == 0)
    def _():
        xn_ref[...] = _rmsnorm_rows(x_ref[...], g_ref[...]).astype(BF16)

    o_ref[...] = _mm(xn_ref[...], w_ref[...]).astype(o_ref.dtype)


def norm_proj(x, gain, w, *, tm=1024, tn=3072):
    m, d = x.shape
    n = w.shape[1]
    return pl.pallas_call(
        _norm_proj_kernel, grid=(m // tm, n // tn),
        in_specs=[pl.BlockSpec((tm, d), lambda i, j: (i, 0)),
                  pl.BlockSpec((1, d), lambda i, j: (0, 0)),
                  pl.BlockSpec((d, tn), lambda i, j: (0, j))],
        out_specs=pl.BlockSpec((tm, tn), lambda i, j: (i, j)),
        out_shape=jax.ShapeDtypeStruct((m, n), BF16),
        scratch_shapes=[pltpu.VMEM((tm, d), BF16)],
        compiler_params=_params("parallel", "arbitrary"),
        name="norm_proj",
    )(x, gain, w)


def _rope_table_kernel(pos_ref, freq_ref, cos_ref, sin_ref):
    ang = pos_ref[0].astype(F32) * freq_ref[...]
    cos_ref[0] = jnp.cos(ang)
    sin_ref[0] = jnp.sin(ang)


def rope_tables(positions, *, ts=512):
    b, s = positions.shape
    half = RET_HEAD_QK // 2
    inv_freq = (ROPE_THETA ** (-jnp.arange(half, dtype=F32) / half)).reshape(1, half)
    out = jax.ShapeDtypeStruct((b, s, half), F32)
    return pl.pallas_call(
        _rope_table_kernel, grid=(b, s // ts),
        in_specs=[pl.BlockSpec((1, ts, 1), lambda i, j: (i, j, 0)),
                  pl.BlockSpec((1, half), lambda i, j: (0, 0))],
        out_specs=[pl.BlockSpec((1, ts, half), lambda i, j: (i, j, 0))] * 2,
        out_shape=[out, out], compiler_params=_params("parallel", "parallel"),
        name="rope_tables",
    )(positions.reshape(b, s, 1), inv_freq)


def _retention_kernel(q_ref, k_ref, v_ref, gate_ref, cos_ref, sin_ref, decay_ref, qdec_ref,
                      kdec_ref, cdec_ref, gain_ref, y_ref, state_ref, *, chunks):
    c = RET_CHUNK
    dk, dv = RET_HEAD_QK, RET_HEAD_V
    half = dk // 2
    heads = range(RET_HEADS)

    @pl.when(pl.program_id(1) == 0)
    def _():
        state_ref[...] = jnp.zeros_like(state_ref)

    def rope(t, cos, sin):
        t1, t2 = t[:, :half], t[:, half:]
        return jnp.concatenate([t1 * cos - t2 * sin, t2 * cos + t1 * sin], axis=-1)

    qd, inner, upd = [], [], []
    for ci in range(chunks):
        rows = pl.ds(ci * c, c)
        cos = cos_ref[0, rows, :]
        sin = sin_ref[0, rows, :]
        for h in heads:
            q = rope(q_ref[0, rows, pl.ds(h * dk, dk)].astype(F32), cos, sin)
            k = rope(k_ref[0, rows, pl.ds(h * dk, dk)].astype(F32), cos, sin) * (dk ** -0.5)
            v = v_ref[0, rows, pl.ds(h * dv, dv)]
            scores = _mm_nt(q.astype(BF16), k.astype(BF16)) * decay_ref[h]
            inner.append(_mm(scores.astype(BF16), v))
            upd.append(_mm_tn((k * kdec_ref[h]).astype(BF16), v))
            qd.append((q * qdec_ref[h]).astype(BF16))

    for ci in range(chunks):
        rows = pl.ds(ci * c, c)
        for h in heads:
            p = ci * RET_HEADS + h
            state = state_ref[h]
            o = inner[p] + _mm(qd[p], state.astype(BF16))
            state_ref[h] = state * cdec_ref[h][0:1, 0:1] + upd[p]
            mu = jnp.mean(o, axis=-1, keepdims=True)
            var = jnp.mean(jnp.square(o - mu), axis=-1, keepdims=True)
            cols = pl.ds(h * dv, dv)
            on = (o - mu) * lax.rsqrt(var + NORM_EPS) * gain_ref[:, cols]
            y_ref[0, rows, cols] = (_silu(gate_ref[0, rows, cols].astype(F32)) * on).astype(y_ref.dtype)


def retention_core(proj, cos, sin, gn_gain, *, chunks=4):
    b, s, _ = proj.shape
    c = RET_CHUNK
    h = RET_HEADS
    rows = c * chunks
    log_gamma = jnp.log1p(-jnp.exp2(-5.0 - jnp.arange(h, dtype=F32)))
    pos = jnp.arange(c, dtype=F32)
    causal = pos[:, None] >= pos[None, :]
    diff = jnp.where(causal, pos[:, None] - pos[None, :], 0.0)
    decay = jnp.where(causal, jnp.exp(log_gamma[:, None, None] * diff), 0.0)
    qdec = jnp.exp(log_gamma[:, None] * (pos + 1.0))[:, :, None]
    kdec = jnp.exp(log_gamma[:, None] * (c - 1.0 - pos))[:, :, None]
    cdec = jnp.broadcast_to(jnp.exp(log_gamma * c)[:, None, None], (h, 8, 128))
    dk, dv = RET_HEAD_QK, RET_HEAD_V
    tile = lambda w, blk: pl.BlockSpec((1, rows, w), lambda bi, ni: (bi, ni, blk))
    whole = lambda *shape: pl.BlockSpec(shape, lambda bi, ni: (0,) * len(shape))
    return pl.pallas_call(
        functools.partial(_retention_kernel, chunks=chunks),
        grid=(b, s // rows),
        in_specs=[
            tile(RET_QK, 0), tile(RET_QK, 1), tile(RET_V, 1), tile(RET_V, 2),
            tile(dk // 2, 0), tile(dk // 2, 0),
            whole(h, c, c), whole(h, c, 1), whole(h, c, 1), whole(h, 8, 128), whole(1, RET_V),
        ],
        out_specs=tile(RET_V, 0),
        out_shape=jax.ShapeDtypeStruct((b, s, RET_V), BF16),
        scratch_shapes=[pltpu.VMEM((h, dk, dv), F32)],
        compiler_params=_params("parallel", "arbitrary"),
        name="retention_core",
    )(proj, proj, proj, proj, cos, sin, decay, qdec, kdec, cdec, gn_gain.reshape(1, RET_V))


def _gdn_proj_kernel(x_ref, xp_ref, g_ref, w_ref, ws_ref, cw_ref, alog_ref, dtb_ref,
                     o_ref, beta_ref, gl_ref, xn_ref, ybuf_ref, *, tiles_per_seq):
    i = pl.program_id(0)
    j = pl.program_id(1)
    p = PREV_ROWS
    gw = GDN_PROJ_GROUP
    hv = GDN_V_HEADS

    @pl.when(j == 0)
    def _():
        first = (i % tiles_per_seq) == 0
        xp = _rmsnorm_rows(xp_ref[...], g_ref[...])
        xn_ref[0:p, :] = jnp.where(first, 0.0, xp).astype(BF16)
        xn = _rmsnorm_rows(x_ref[...], g_ref[...]).astype(BF16)
        xn_ref[p:, :] = xn
        ba = _mm(xn, ws_ref[...])
        beta_ref[...] = _sigmoid(ba[:, :hv])
        a = ba[:, hv:] + dtb_ref[...]
        softplus = jnp.maximum(a, 0.0) + jnp.log1p(jnp.exp(-jnp.abs(a)))
        gl_ref[...] = -jnp.exp(alog_ref[...]) * softplus

    groups = o_ref.shape[1] // gw
    tm = x_ref.shape[0]

    def project(group):
        return _mm(xn_ref[...], w_ref[:, pl.ds(group * gw, gw)])

    def conv_silu(y, group):
        cw = cw_ref[:, pl.ds(group * gw, gw)]
        ybuf = ybuf_ref.at[group % 2]
        ybuf[...] = y
        acc = y[p:] * cw[GDN_CONV_K - 1:GDN_CONV_K, :]
        for sh in range(1, GDN_CONV_K):
            acc = acc + ybuf[pl.ds(p - sh, tm), :] * cw[GDN_CONV_K - 1 - sh:GDN_CONV_K - sh, :]
        return _silu(acc)

    def l2n(y, scale):
        return y * (lax.rsqrt(jnp.sum(y * y, axis=-1, keepdims=True) + NORM_EPS) * scale)

    def conv_groups(finish):
        y_next = project(0)
        for group in range(groups):
            y = y_next
            if group + 1 < groups:
                y_next = project(group + 1)
            o_ref[:, pl.ds(group * gw, gw)] = finish(conv_silu(y, group), group).astype(o_ref.dtype)

    def qk_finish(s, group):
        scale = GDN_HEAD ** -0.5 if group * gw < GDN_QK else 1.0
        return jnp.concatenate([l2n(s[:, c0:c0 + GDN_HEAD], scale) for c0 in range(0, gw, GDN_HEAD)], axis=-1)

    @pl.when(j == 0)
    def _():
        conv_groups(qk_finish)

    @pl.when(j == 1)
    def _():
        conv_groups(lambda s, group: s)

    @pl.when(j == 2)
    def _():
        o_ref[...] = _mm(xn_ref[p:, :], w_ref[...]).astype(o_ref.dtype)


def gdn_proj(x, gain, w_main, w_small, conv_w, a_log, dt_bias, *, seq, tm=1024):
    m, d = x.shape
    hv = GDN_V_HEADS
    tn = GDN_V
    assert GDN_CONV_DIM == 2 * tn and 2 * GDN_QK == tn and seq % tm == 0
    small = lambda shape: pl.BlockSpec(shape, lambda i, j: (0, 0))
    return pl.pallas_call(
        functools.partial(_gdn_proj_kernel, tiles_per_seq=seq // tm),
        grid=(m // tm, GDN_MAIN // tn),
        in_specs=[
            pl.BlockSpec((tm, d), lambda i, j: (i, 0)),
            pl.BlockSpec((PREV_ROWS, d), lambda i, j: (jnp.maximum(i * (tm // PREV_ROWS) - 1, 0), 0)),
            small((1, d)),
            pl.BlockSpec((d, tn), lambda i, j: (0, j)),
            small((d, 2 * hv)),
            pl.BlockSpec((GDN_CONV_K, tn), lambda i, j: (0, jnp.minimum(j, 1))),
            small((1, hv)), small((1, hv)),
        ],
        out_specs=[pl.BlockSpec((tm, tn), lambda i, j: (i, j)),
                   pl.BlockSpec((tm, hv), lambda i, j: (i, 0)),
                   pl.BlockSpec((tm, hv), lambda i, j: (i, 0))],
        out_shape=[jax.ShapeDtypeStruct((m, GDN_MAIN), BF16),
                   jax.ShapeDtypeStruct((m, hv), F32), jax.ShapeDtypeStruct((m, hv), F32)],
        scratch_shapes=[pltpu.VMEM((PREV_ROWS + tm, d), BF16),
                        pltpu.VMEM((2, PREV_ROWS + tm, GDN_PROJ_GROUP), F32)],
        compiler_params=_params("parallel", "arbitrary"),
        name="gdn_proj",
    )(x, x, gain, w_main, w_small, conv_w, a_log.reshape(1, hv), dt_bias.reshape(1, hv))


def _unit_lower_inverses(a_list, eye, base_mask, level_masks):
    d = [jnp.where(base_mask, a, 0.0) for a in a_list]
    db = _bf(d)
    d2 = [_mm(x, x) for x in db]
    d2b = _bf(d2)
    p = [eye - x for x in d]
    p = [pi + _mm(pi.astype(BF16), x2) for pi, x2 in zip(p, d2b)]
    d4b = _bf([_mm(x, x) for x in d2b])
    t = [pi + _mm(pi.astype(BF16), x4) for pi, x4 in zip(p, d4b)]
    for mask in level_masks:
        eb = _bf([jnp.where(mask, a, 0.0) for a in a_list])
        tb = _bf(t)
        teb = _bf([_mm(ti, ei) for ti, ei in zip(tb, eb)])
        t = [ti - _mm(tei, tbi) for ti, tei, tbi in zip(t, teb, tb)]
    return t


def _gdn_core_kernel(q_ref, k_ref, v_ref, beta_ref, g_ref, z_ref, gain_ref, y_ref, state_ref, *, chunks, sub):
    c = GDN_CHUNK
    dh = GDN_HEAD
    nk = GDN_K_HEADS
    nv = GDN_V_HEADS
    rep = nv // nk

    @pl.when(pl.program_id(1) == 0)
    def _():
        state_ref[...] = jnp.zeros_like(state_ref)

    ri = lax.broadcasted_iota(jnp.int32, (c, c), 0)
    ci = lax.broadcasted_iota(jnp.int32, (c, c), 1)
    causal = ri >= ci
    strict = ri > ci
    eye = (ri == ci).astype(F32)
    tril = causal.astype(F32)
    same = lambda size: (ri // size) == (ci // size)
    base_mask = same(GDN_INV_BLOCK)
    level_masks = []
    size = GDN_INV_BLOCK
    while size < c:
        level_masks.append(same(2 * size) & jnp.logical_not(same(size)))
        size *= 2
    gain = gain_ref[...]

    def group_step(idx, carry):
        rows = [pl.ds(pl.multiple_of((idx * sub + s_) * c, c), c) for s_ in range(sub)]
        heads = range(nv)
        vprobs = [(s_, h) for s_ in range(sub) for h in heads]
        kprobs = [(s_, h) for s_ in range(sub) for h in range(nk)]
        col = lambda x, h: x[:, h:h + 1]

        gcum = [_hdot(tril, g_ref[0, r, :]) for r in rows]
        beta = [beta_ref[0, r, :] for r in rows]
        glast = [x[c - 1:c, :] for x in gcum]
        eg = [jnp.exp(x) for x in gcum]
        kdec = [jnp.exp(gl - x) for gl, x in zip(glast, gcum)]
        eglast = [jnp.exp(gl) for gl in glast]

        q16 = {(s_, h): q_ref[0, rows[s_], pl.ds(h * dh, dh)] for s_, h in kprobs}
        k16 = {(s_, h): k_ref[0, rows[s_], pl.ds(h * dh, dh)] for s_, h in kprobs}
        q = {p: q16[p].astype(F32) for p in kprobs}
        k = {p: k16[p].astype(F32) for p in kprobs}
        kk = {p: _mm_nt(k16[p], k16[p]) for p in kprobs}
        qk = {p: _mm_nt(q16[p], k16[p]) for p in kprobs}

        decay, a = {}, []
        for s_, h in vprobs:
            gcol = col(gcum[s_], h)
            grow = jnp.sum(gcol * eye, axis=0, keepdims=True)
            decay[s_, h] = jnp.where(causal, jnp.exp(jnp.where(causal, gcol - grow, 0.0)), 0.0)
            a.append(jnp.where(strict, kk[s_, h // rep] * col(beta[s_], h) * decay[s_, h], 0.0))
        t = _unit_lower_inverses(a, eye, base_mask, level_masks)

        rhs = []
        for s_, h in vprobs:
            b_col = col(beta[s_], h)
            v = v_ref[0, rows[s_], pl.ds(h * dh, dh)].astype(F32)
            rhs.append(jnp.concatenate([v * b_col, k[s_, h // rep] * (b_col * col(eg[s_], h))],
                                       axis=-1).astype(BF16))
        sol = dict(zip(vprobs, [_mm(ti, ri_) for ti, ri_ in zip(_bf(t), rhs)]))
        attn = {(s_, h): jnp.where(causal, qk[s_, h // rep] * decay[s_, h], 0.0).astype(BF16)
                for s_, h in vprobs}
        wq = {(s_, h): jnp.concatenate([sol[s_, h][:, dh:], q[s_, h // rep] * col(eg[s_], h)],
                                       axis=0).astype(BF16) for s_, h in vprobs}
        kd = {(s_, h): (k[s_, h // rep] * col(kdec[s_], h)).astype(BF16) for s_, h in vprobs}

        state = [state_ref[h] for h in heads]
        for s_ in range(sub):
            ws_qs = [_mm(wq[s_, h], state[h].astype(BF16)) for h in heads]
            v_new = _bf([sol[s_, h][:, :dh] - ws_qs[h][:c] for h in heads])
            o = [ws_qs[h][c:] + _mm(attn[s_, h], v_new[h]) for h in heads]
            state = [state[h] * eglast[s_][:, h:h + 1] + _mm_tn(kd[s_, h], v_new[h]) for h in heads]
            for h in heads:
                on = o[h] * lax.rsqrt(jnp.mean(o[h] * o[h], axis=-1, keepdims=True) + NORM_EPS) * gain
                cols = pl.ds(h * dh, dh)
                y_ref[0, rows[s_], cols] = (on * _silu(z_ref[0, rows[s_], cols].astype(F32))).astype(y_ref.dtype)
        for h in heads:
            state_ref[h] = state[h]
        return carry

    lax.fori_loop(0, chunks // sub, group_step, 0)


def gdn_core(proj, beta, g, norm_gain, *, chunks=4, sub=2):
    b, s, _ = proj.shape
    rows = GDN_CHUNK * chunks
    hv = GDN_V_HEADS
    tile = lambda w, blk=0: pl.BlockSpec((1, rows, w), lambda bi, ni: (bi, ni, blk))
    return pl.pallas_call(
        functools.partial(_gdn_core_kernel, chunks=chunks, sub=sub), grid=(b, s // rows),
        in_specs=[tile(GDN_QK, 0), tile(GDN_QK, 1), tile(GDN_V, 1), tile(hv), tile(hv),
                  tile(GDN_V, 2),
                  pl.BlockSpec((1, GDN_HEAD), lambda bi, ni: (0, 0))],
        out_specs=tile(GDN_V),
        out_shape=jax.ShapeDtypeStruct((b, s, GDN_V), BF16),
        scratch_shapes=[pltpu.VMEM((hv, GDN_HEAD, GDN_HEAD), F32)],
        compiler_params=_params("parallel", "arbitrary"),
        name="gdn_core",
    )(proj, proj, proj, beta, g, proj, norm_gain.reshape(1, GDN_HEAD))


def _mix_ffn_kernel(h_ref, y_ref, wmix_ref, g_ref, win_ref, wout_ref, fg_ref, o_ref, *, final_norm):
    h1 = h_ref[...] + _mm(y_ref[...], wmix_ref[...])
    xn = _rmsnorm_rows(h1, g_ref[...]).astype(BF16)
    gate = _mm(xn, win_ref[:, :FFN_HIDDEN])
    up = _mm(xn, win_ref[:, FFN_HIDDEN:])
    act = (_silu(gate) * up).astype(BF16)
    out = h1 + _mm(act, wout_ref[...])
    if final_norm:
        out = _rmsnorm_rows(out, fg_ref[...])
    o_ref[...] = out


def mix_ffn_residual(h, y, w_mix, gain, w_in, w_out, final_gain, *, final_norm, tm=256):
    m, d = h.shape
    kdim = y.shape[1]
    resident = functools.partial(pl.BlockSpec, index_map=lambda i: (0, 0), pipeline_mode=pl.Buffered(1))
    rows = lambda w: pl.BlockSpec((tm, w), lambda i: (i, 0))
    return pl.pallas_call(
        functools.partial(_mix_ffn_kernel, final_norm=final_norm),
        grid=(m // tm,),
        in_specs=[rows(d), rows(kdim), resident((kdim, d)), resident((1, d)),
                  resident((d, 2 * FFN_HIDDEN)), resident((FFN_HIDDEN, d)), resident((1, d))],
        out_specs=rows(d),
        out_shape=jax.ShapeDtypeStruct((m, d), F32),
        compiler_params=_params("parallel"),
        name="mix_ffn_residual",
    )(h, y, w_mix, gain, w_in, w_out, final_gain)


def kernel(x, positions, norm_mix, norm_ffn, norm_final, ret_w_in, ret_gn_gain, ret_w_out,
           gdn_w_in, gdn_conv, gdn_a_log, gdn_dt_bias, gdn_norm_gain, gdn_w_out,
           ffn_w_in, ffn_w_out):
    b, s, d = x.shape
    m = b * s
    h = x.reshape(m, d)
    cos, sin = rope_tables(positions)
    final_gain = norm_final.reshape(1, d)
    for i in range(DEPTH):
        j = i // N_MIXERS
        gain = norm_mix[i].reshape(1, d)
        if i % N_MIXERS == 0:
            proj = norm_proj(h, gain, ret_w_in[j].astype(BF16))
            y = retention_core(proj.reshape(b, s, RET_IN), cos, sin, ret_gn_gain[j])
            w_out = ret_w_out[j]
        else:
            w_in = gdn_w_in[j]
            proj, beta, g = gdn_proj(h, gain, w_in[:, :GDN_MAIN].astype(BF16), w_in[:, GDN_MAIN:].astype(BF16),
                                     gdn_conv[j], gdn_a_log[j], gdn_dt_bias[j], seq=s)
            hv = GDN_V_HEADS
            y = gdn_core(proj.reshape(b, s, GDN_MAIN), beta.reshape(b, s, hv), g.reshape(b, s, hv),
                         gdn_norm_gain[j])
            w_out = gdn_w_out[j]
        h = mix_ffn_residual(h, y.reshape(m, -1), w_out.astype(BF16), norm_ffn[i].reshape(1, d),
                             ffn_w_in[i].astype(BF16), ffn_w_out[i].astype(BF16), final_gain,
                             final_norm=(i == DEPTH - 1))
    return h.reshape(b, s, d)
```

```python
import functools

import jax
import jax.numpy as jnp
from jax import lax
from jax.experimental import pallas as pl
from jax.experimental.pallas import tpu as pltpu

F32 = jnp.float32
BF16 = jnp.bfloat16
HIGHEST = lax.Precision.HIGHEST

D_MODEL = 1024
DEPTH = 4
N_MIXERS = 2

RET_HEADS = 4
RET_HEAD_QK = D_MODEL // RET_HEADS
RET_HEAD_V = 2 * RET_HEAD_QK
RET_QK = RET_HEADS * RET_HEAD_QK
RET_V = RET_HEADS * RET_HEAD_V
RET_IN = 2 * RET_QK + 2 * RET_V
RET_CHUNK = 128
ROPE_THETA = 10000.0

GDN_HEAD = 128
GDN_K_HEADS = D_MODEL // GDN_HEAD
GDN_V_HEADS = 2 * GDN_K_HEADS
GDN_QK = GDN_K_HEADS * GDN_HEAD
GDN_V = GDN_V_HEADS * GDN_HEAD
GDN_CONV_DIM = 2 * GDN_QK + GDN_V
GDN_MAIN = GDN_CONV_DIM + GDN_V
GDN_CONV_K = 4
GDN_CHUNK = 64
GDN_INV_BLOCK = 8
PREV_ROWS = 16
GDN_PROJ_GROUP = 256

FFN_HIDDEN = 2816
NORM_EPS = 1e-6

VMEM_LIMIT = 48 * 1024 * 1024
VMEM_LIMIT_FFN = 56 * 1024 * 1024


def _params(*sem, vmem=VMEM_LIMIT):
    return pltpu.CompilerParams(dimension_semantics=sem, vmem_limit_bytes=vmem)


def _mm(a, b):
    return jnp.dot(a, b, preferred_element_type=F32)


def _mm_nt(a, b):
    return lax.dot_general(a, b, (((1,), (1,)), ((), ())), preferred_element_type=F32)


def _mm_tn(a, b):
    return lax.dot_general(a, b, (((0,), (0,)), ((), ())), preferred_element_type=F32)


def _hdot(a, b):
    return jnp.dot(a, b, precision=HIGHEST, preferred_element_type=F32)


def _bf(xs):
    return [x.astype(BF16) for x in xs]


def _sigmoid(x):
    return 1.0 / (1.0 + jnp.exp(-x))


def _silu(x):
    half = 0.5 * x
    return half + half * jnp.tanh(half)


def _rmsnorm_rows(x, gain):
    return x * lax.rsqrt(jnp.mean(x * x, axis=-1, keepdims=True) + NORM_EPS) * gain


def _norm_proj_kernel(x_ref, g_ref, w_ref, o_ref, xn_ref):
    @pl.when(pl.program_id(1) == 0)
    def _():
        xn_ref[...] = _rmsnorm_rows(x_ref[...], g_ref[...]).astype(BF16)

    o_ref[...] = _mm(xn_ref[...], w_ref[...]).astype(o_ref.dtype)


def norm_proj(x, gain, w, layer, *, tm=1024, tn=3072):
    m, d = x.shape
    n = w.shape[2]
    return pl.pallas_call(
        _norm_proj_kernel, grid=(m // tm, n // tn),
        in_specs=[pl.BlockSpec((tm, d), lambda i, j: (i, 0)),
                  pl.BlockSpec((1, d), lambda i, j: (0, 0)),
                  pl.BlockSpec((None, d, tn), lambda i, j: (layer, 0, j))],
        out_specs=pl.BlockSpec((tm, tn), lambda i, j: (i, j)),
        out_shape=jax.ShapeDtypeStruct((m, n), BF16),
        scratch_shapes=[pltpu.VMEM((tm, d), BF16)],
        compiler_params=_params("parallel", "arbitrary"),
        name="norm_proj",
    )(x, gain, w)


def _rope_table_kernel(pos_ref, freq_ref, cos_ref, sin_ref):
    ang = pos_ref[0].astype(F32) * freq_ref[...]
    cos_ref[0] = jnp.cos(ang)
    sin_ref[0] = jnp.sin(ang)


def rope_tables(positions, *, ts=512):
    b, s = positions.shape
    half = RET_HEAD_QK // 2
    inv_freq = (ROPE_THETA ** (-jnp.arange(half, dtype=F32) / half)).reshape(1, half)
    out = jax.ShapeDtypeStruct((b, s, half), F32)
    return pl.pallas_call(
        _rope_table_kernel, grid=(b, s // ts),
        in_specs=[pl.BlockSpec((1, ts, 1), lambda i, j: (i, j, 0)),
                  pl.BlockSpec((1, half), lambda i, j: (0, 0))],
        out_specs=[pl.BlockSpec((1, ts, half), lambda i, j: (i, j, 0))] * 2,
        out_shape=[out, out], compiler_params=_params("parallel", "parallel"),
        name="rope_tables",
    )(positions.reshape(b, s, 1), inv_freq)


def _retention_kernel(q_ref, k_ref, v_ref, gate_ref, cos_ref, sin_ref, decay_ref, qdec_ref,
                      kdec_ref, cdec_ref, gain_ref, y_ref, state_ref, *, chunks):
    c = RET_CHUNK
    dk, dv = RET_HEAD_QK, RET_HEAD_V
    half = dk // 2
    heads = range(RET_HEADS)

    @pl.when(pl.program_id(1) == 0)
    def _():
        state_ref[...] = jnp.zeros_like(state_ref)

    def rope(t, cos, sin):
        t1, t2 = t[:, :half], t[:, half:]
        return jnp.concatenate([t1 * cos - t2 * sin, t2 * cos + t1 * sin], axis=-1)

    qd, inner, upd = [], [], []
    for ci in range(chunks):
        rows = pl.ds(ci * c, c)
        cos = cos_ref[0, rows, :]
        sin = sin_ref[0, rows, :]
        for h in heads:
            q = rope(q_ref[0, rows, pl.ds(h * dk, dk)].astype(F32), cos, sin)
            k = rope(k_ref[0, rows, pl.ds(h * dk, dk)].astype(F32), cos, sin) * (dk ** -0.5)
            v = v_ref[0, rows, pl.ds(h * dv, dv)]
            scores = _mm_nt(q.astype(BF16), k.astype(BF16)) * decay_ref[h]
            inner.append(_mm(scores.astype(BF16), v))
            upd.append(_mm_tn((k * kdec_ref[h]).astype(BF16), v))
            qd.append((q * qdec_ref[h]).astype(BF16))

    for ci in range(chunks):
        rows = pl.ds(ci * c, c)
        for h in heads:
            p = ci * RET_HEADS + h
            state = state_ref[h]
            o = inner[p] + _mm(qd[p], state.astype(BF16))
            state_ref[h] = state * cdec_ref[h][0:1, 0:1] + upd[p]
            mu = jnp.mean(o, axis=-1, keepdims=True)
            var = jnp.mean(jnp.square(o - mu), axis=-1, keepdims=True)
            cols = pl.ds(h * dv, dv)
            on = (o - mu) * lax.rsqrt(var + NORM_EPS) * gain_ref[:, cols]
            y_ref[0, rows, cols] = (_silu(gate_ref[0, rows, cols].astype(F32)) * on).astype(y_ref.dtype)


def retention_core(proj, cos, sin, gn_gain, *, chunks=4):
    b, s, _ = proj.shape
    c = RET_CHUNK
    h = RET_HEADS
    rows = c * chunks
    log_gamma = jnp.log1p(-jnp.exp2(-5.0 - jnp.arange(h, dtype=F32)))
    pos = jnp.arange(c, dtype=F32)
    causal = pos[:, None] >= pos[None, :]
    diff = jnp.where(causal, pos[:, None] - pos[None, :], 0.0)
    decay = jnp.where(causal, jnp.exp(log_gamma[:, None, None] * diff), 0.0)
    qdec = jnp.exp(log_gamma[:, None] * (pos + 1.0))[:, :, None]
    kdec = jnp.exp(log_gamma[:, None] * (c - 1.0 - pos))[:, :, None]
    cdec = jnp.broadcast_to(jnp.exp(log_gamma * c)[:, None, None], (h, 8, 128))
    dk, dv = RET_HEAD_QK, RET_HEAD_V
    tile = lambda w, blk: pl.BlockSpec((1, rows, w), lambda bi, ni: (bi, ni, blk))
    whole = lambda *shape: pl.BlockSpec(shape, lambda bi, ni: (0,) * len(shape))
    return pl.pallas_call(
        functools.partial(_retention_kernel, chunks=chunks),
        grid=(b, s // rows),
        in_specs=[
            tile(RET_QK, 0), tile(RET_QK, 1), tile(RET_V, 1), tile(RET_V, 2),
            tile(dk // 2, 0), tile(dk // 2, 0),
            whole(h, c, c), whole(h, c, 1), whole(h, c, 1), whole(h, 8, 128), whole(1, RET_V),
        ],
        out_specs=tile(RET_V, 0),
        out_shape=jax.ShapeDtypeStruct((b, s, RET_V), BF16),
        scratch_shapes=[pltpu.VMEM((h, dk, dv), F32)],
        compiler_params=_params("parallel", "arbitrary"),
        name="retention_core",
    )(proj, proj, proj, proj, cos, sin, decay, qdec, kdec, cdec, gn_gain.reshape(1, RET_V))


def _gdn_proj_kernel(x_ref, xp_ref, g_ref, w_ref, ws_ref, cw_ref, alog_ref, dtb_ref,
                     o_ref, beta_ref, gl_ref, xn_ref, ybuf_ref, *, tiles_per_seq):
    i = pl.program_id(0)
    j = pl.program_id(1)
    p = PREV_ROWS
    gw = GDN_PROJ_GROUP
    hv = GDN_V_HEADS

    @pl.when(j == 0)
    def _():
        first = (i % tiles_per_seq) == 0
        xp = _rmsnorm_rows(xp_ref[...], g_ref[...])
        xn_ref[0:p, :] = jnp.where(first, 0.0, xp).astype(BF16)
        xn = _rmsnorm_rows(x_ref[...], g_ref[...]).astype(BF16)
        xn_ref[p:, :] = xn
        ba = _mm(xn, ws_ref[...])
        beta_ref[...] = _sigmoid(ba[:, :hv])
        a = ba[:, hv:] + dtb_ref[...]
        softplus = jnp.maximum(a, 0.0) + jnp.log1p(jnp.exp(-jnp.abs(a)))
        gl_ref[...] = -jnp.exp(alog_ref[...]) * softplus

    groups = o_ref.shape[1] // gw
    tm = x_ref.shape[0]

    def project(group):
        return _mm(xn_ref[...], w_ref[:, pl.ds(group * gw, gw)])

    def conv_silu(y, group):
        cw = cw_ref[:, pl.ds(group * gw, gw)]
        ybuf = ybuf_ref.at[group % 2]
        ybuf[...] = y
        acc = y[p:] * cw[GDN_CONV_K - 1:GDN_CONV_K, :]
        for sh in range(1, GDN_CONV_K):
            acc = acc + ybuf[pl.ds(p - sh, tm), :] * cw[GDN_CONV_K - 1 - sh:GDN_CONV_K - sh, :]
        return _silu(acc)

    def l2n(y, scale):
        return y * (lax.rsqrt(jnp.sum(y * y, axis=-1, keepdims=True) + NORM_EPS) * scale)

    def conv_groups(finish):
        y_next = project(0)
        for group in range(groups):
            y = y_next
            if group + 1 < groups:
                y_next = project(group + 1)
            o_ref[:, pl.ds(group * gw, gw)] = finish(conv_silu(y, group), group).astype(o_ref.dtype)

    def qk_finish(s, group):
        scale = GDN_HEAD ** -0.5 if group * gw < GDN_QK else 1.0
        return jnp.concatenate([l2n(s[:, c0:c0 + GDN_HEAD], scale) for c0 in range(0, gw, GDN_HEAD)], axis=-1)

    @pl.when(j == 0)
    def _():
        conv_groups(qk_finish)

    @pl.when(j == 1)
    def _():
        conv_groups(lambda s, group: s)

    @pl.when(j == 2)
    def _():
        o_ref[...] = _mm(xn_ref[p:, :], w_ref[...]).astype(o_ref.dtype)


def gdn_proj(x, gain, w_in, w_small, layer, conv_w, a_log, dt_bias, *, seq, tm=1024):
    m, d = x.shape
    hv = GDN_V_HEADS
    tn = GDN_V
    assert GDN_CONV_DIM == 2 * tn and 2 * GDN_QK == tn and seq % tm == 0
    small = lambda shape: pl.BlockSpec(shape, lambda i, j: (0, 0))
    return pl.pallas_call(
        functools.partial(_gdn_proj_kernel, tiles_per_seq=seq // tm),
        grid=(m // tm, GDN_MAIN // tn),
        in_specs=[
            pl.BlockSpec((tm, d), lambda i, j: (i, 0)),
            pl.BlockSpec((PREV_ROWS, d), lambda i, j: (jnp.maximum(i * (tm // PREV_ROWS) - 1, 0), 0)),
            small((1, d)),
            pl.BlockSpec((None, d, tn), lambda i, j: (layer, 0, j)),
            small((d, 2 * hv)),
            pl.BlockSpec((GDN_CONV_K, tn), lambda i, j: (0, jnp.minimum(j, 1))),
            small((1, hv)), small((1, hv)),
        ],
        out_specs=[pl.BlockSpec((tm, tn), lambda i, j: (i, j)),
                   pl.BlockSpec((tm, hv), lambda i, j: (i, 0)),
                   pl.BlockSpec((tm, hv), lambda i, j: (i, 0))],
        out_shape=[jax.ShapeDtypeStruct((m, GDN_MAIN), BF16),
                   jax.ShapeDtypeStruct((m, hv), F32), jax.ShapeDtypeStruct((m, hv), F32)],
        scratch_shapes=[pltpu.VMEM((PREV_ROWS + tm, d), BF16),
                        pltpu.VMEM((2, PREV_ROWS + tm, GDN_PROJ_GROUP), F32)],
        compiler_params=_params("parallel", "arbitrary"),
        name="gdn_proj",
    )(x, x, gain, w_in, w_small, conv_w, a_log.reshape(1, hv), dt_bias.reshape(1, hv))


def _unit_lower_inverses(a_list, eye, base_mask, level_masks):
    d = [jnp.where(base_mask, a, 0.0) for a in a_list]
    db = _bf(d)
    d2b = _bf([_mm(x, x) for x in db])
    p = [eye - x for x in d]
    p = [pi + _mm(pi.astype(BF16), x2) for pi, x2 in zip(p, d2b)]
    d4b = _bf([_mm(x, x) for x in d2b])
    t = [pi + _mm(pi.astype(BF16), x4) for pi, x4 in zip(p, d4b)]
    for mask in level_masks:
        eb = _bf([jnp.where(mask, a, 0.0) for a in a_list])
        tb = _bf(t)
        teb = _bf([_mm(ti, ei) for ti, ei in zip(tb, eb)])
        t = [ti - _mm(tei, tbi) for ti, tei, tbi in zip(t, teb, tb)]
    return t


def _gdn_core_kernel(q_ref, k_ref, v_ref, beta_ref, g_ref, z_ref, gain_ref, y_ref, state_ref, *, chunks, sub):
    c = GDN_CHUNK
    dh = GDN_HEAD
    nk = GDN_K_HEADS
    nv = GDN_V_HEADS
    rep = nv // nk

    @pl.when(pl.program_id(1) == 0)
    def _():
        state_ref[...] = jnp.zeros_like(state_ref)

    ri = lax.broadcasted_iota(jnp.int32, (c, c), 0)
    ci = lax.broadcasted_iota(jnp.int32, (c, c), 1)
    causal = ri >= ci
    strict = ri > ci
    eye = (ri == ci).astype(F32)
    tril = causal.astype(F32)
    same = lambda size: (ri // size) == (ci // size)
    base_mask = same(GDN_INV_BLOCK)
    level_masks = []
    size = GDN_INV_BLOCK
    while size < c:
        level_masks.append(same(2 * size) & jnp.logical_not(same(size)))
        size *= 2
    gain = gain_ref[...]

    def group_step(idx, carry):
        rows = [pl.ds(pl.multiple_of((idx * sub + s_) * c, c), c) for s_ in range(sub)]
        heads = range(nv)
        vprobs = [(s_, h) for s_ in range(sub) for h in heads]
        kprobs = [(s_, h) for s_ in range(sub) for h in range(nk)]
        col = lambda x, h: x[:, h:h + 1]

        gcum = [_hdot(tril, g_ref[0, r, :]) for r in rows]
        beta = [beta_ref[0, r, :] for r in rows]
        glast = [x[c - 1:c, :] for x in gcum]
        eg = [jnp.exp(x) for x in gcum]
        kdec = [jnp.exp(gl - x) for gl, x in zip(glast, gcum)]
        eglast = [jnp.exp(gl) for gl in glast]

        q16 = {(s_, h): q_ref[0, rows[s_], pl.ds(h * dh, dh)] for s_, h in kprobs}
        k16 = {(s_, h): k_ref[0, rows[s_], pl.ds(h * dh, dh)] for s_, h in kprobs}
        q = {p: q16[p].astype(F32) for p in kprobs}
        k = {p: k16[p].astype(F32) for p in kprobs}
        kk = {p: _mm_nt(k16[p], k16[p]) for p in kprobs}
        qk = {p: _mm_nt(q16[p], k16[p]) for p in kprobs}

        decay, a = {}, []
        for s_, h in vprobs:
            gcol = col(gcum[s_], h)
            grow = jnp.sum(gcol * eye, axis=0, keepdims=True)
            decay[s_, h] = jnp.where(causal, jnp.exp(jnp.where(causal, gcol - grow, 0.0)), 0.0)
            a.append(jnp.where(strict, kk[s_, h // rep] * col(beta[s_], h) * decay[s_, h], 0.0))
        t = _unit_lower_inverses(a, eye, base_mask, level_masks)

        rhs = []
        for s_, h in vprobs:
            b_col = col(beta[s_], h)
            v = v_ref[0, rows[s_], pl.ds(h * dh, dh)].astype(F32)
            rhs.append(jnp.concatenate([v * b_col, k[s_, h // rep] * (b_col * col(eg[s_], h))],
                                       axis=-1).astype(BF16))
        sol = dict(zip(vprobs, [_mm(ti, ri_) for ti, ri_ in zip(_bf(t), rhs)]))
        attn = {(s_, h): jnp.where(causal, qk[s_, h // rep] * decay[s_, h], 0.0).astype(BF16)
                for s_, h in vprobs}
        wq = {(s_, h): jnp.concatenate([sol[s_, h][:, dh:], q[s_, h // rep] * col(eg[s_], h)],
                                       axis=0).astype(BF16) for s_, h in vprobs}
        kd = {(s_, h): (k[s_, h // rep] * col(kdec[s_], h)).astype(BF16) for s_, h in vprobs}

        state = [state_ref[h] for h in heads]
        for s_ in range(sub):
            ws_qs = [_mm(wq[s_, h], state[h].astype(BF16)) for h in heads]
            v_new = _bf([sol[s_, h][:, :dh] - ws_qs[h][:c] for h in heads])
            o = [ws_qs[h][c:] + _mm(attn[s_, h], v_new[h]) for h in heads]
            state = [state[h] * eglast[s_][:, h:h + 1] + _mm_tn(kd[s_, h], v_new[h]) for h in heads]
            for h in heads:
                on = o[h] * lax.rsqrt(jnp.mean(o[h] * o[h], axis=-1, keepdims=True) + NORM_EPS) * gain
                cols = pl.ds(h * dh, dh)
                y_ref[0, rows[s_], cols] = (on * _silu(z_ref[0, rows[s_], cols].astype(F32))).astype(y_ref.dtype)
        for h in heads:
            state_ref[h] = state[h]
        return carry

    lax.fori_loop(0, chunks // sub, group_step, 0)


def gdn_core(proj, beta, g, norm_gain, *, chunks=8, sub=2):
    b, s, _ = proj.shape
    rows = GDN_CHUNK * chunks
    hv = GDN_V_HEADS
    tile = lambda w, blk=0: pl.BlockSpec((1, rows, w), lambda bi, ni: (bi, ni, blk))
    return pl.pallas_call(
        functools.partial(_gdn_core_kernel, chunks=chunks, sub=sub), grid=(b, s // rows),
        in_specs=[tile(GDN_QK, 0), tile(GDN_QK, 1), tile(GDN_V, 1), tile(hv), tile(hv),
                  tile(GDN_V, 2),
                  pl.BlockSpec((1, GDN_HEAD), lambda bi, ni: (0, 0))],
        out_specs=tile(GDN_V),
        out_shape=jax.ShapeDtypeStruct((b, s, GDN_V), BF16),
        scratch_shapes=[pltpu.VMEM((hv, GDN_HEAD, GDN_HEAD), F32)],
        compiler_params=_params("parallel", "arbitrary"),
        name="gdn_core",
    )(proj, proj, proj, beta, g, proj, norm_gain.reshape(1, GDN_HEAD))


def _mix_ffn_kernel(h_ref, y_ref, wmix_ref, g_ref, win_ref, wout_ref, fg_ref, o_ref, *, final_norm):
    h1 = h_ref[...] + _mm(y_ref[...], wmix_ref[...])
    xn = _rmsnorm_rows(h1, g_ref[...]).astype(BF16)
    gate = _mm(xn, win_ref[:, :FFN_HIDDEN])
    up = _mm(xn, win_ref[:, FFN_HIDDEN:])
    act = (_silu(gate) * up).astype(BF16)
    out = h1 + _mm(act, wout_ref[...])
    if final_norm:
        out = _rmsnorm_rows(out, fg_ref[...])
    o_ref[...] = out


def mix_ffn_residual(h, y, w_mix, mix_layer, gain, w_in, w_out, ffn_layer, final_gain, *, final_norm, tm=512):
    m, d = h.shape
    kdim = y.shape[1]
    resident = functools.partial(pl.BlockSpec, index_map=lambda i: (0, 0), pipeline_mode=pl.Buffered(1))
    layer_of = lambda layer, *shape: pl.BlockSpec((None,) + shape, lambda i: (layer, 0, 0),
                                                  pipeline_mode=pl.Buffered(1))
    rows = lambda w: pl.BlockSpec((tm, w), lambda i: (i, 0))
    return pl.pallas_call(
        functools.partial(_mix_ffn_kernel, final_norm=final_norm),
        grid=(m // tm,),
        in_specs=[rows(d), rows(kdim), layer_of(mix_layer, kdim, d), resident((1, d)),
                  layer_of(ffn_layer, d, 2 * FFN_HIDDEN), layer_of(ffn_layer, FFN_HIDDEN, d), resident((1, d))],
        out_specs=rows(d),
        out_shape=jax.ShapeDtypeStruct((m, d), F32),
        compiler_params=_params("parallel", vmem=VMEM_LIMIT_FFN),
        name="mix_ffn_residual",
    )(h, y, w_mix, gain, w_in, w_out, final_gain)


def kernel(x, positions, norm_mix, norm_ffn, norm_final, ret_w_in, ret_gn_gain, ret_w_out,
           gdn_w_in, gdn_conv, gdn_a_log, gdn_dt_bias, gdn_norm_gain, gdn_w_out,
           ffn_w_in, ffn_w_out):
    b, s, d = x.shape
    m = b * s
    h = x.reshape(m, d)
    cos, sin = rope_tables(positions)
    final_gain = norm_final.reshape(1, d)
    ret_w_in, ret_w_out, gdn_w_out, ffn_w_in, ffn_w_out = (
        w.astype(BF16) for w in (ret_w_in, ret_w_out, gdn_w_out, ffn_w_in, ffn_w_out))
    gdn_w_small = gdn_w_in[:, :, GDN_MAIN:].astype(BF16)
    gdn_w_in = gdn_w_in.astype(BF16)
    hv = GDN_V_HEADS
    for i in range(DEPTH):
        j = i // N_MIXERS
        gain = norm_mix[i].reshape(1, d)
        if i % N_MIXERS == 0:
            proj = norm_proj(h, gain, ret_w_in, j)
            y = retention_core(proj.reshape(b, s, RET_IN), cos, sin, ret_gn_gain[j])
            w_mix = ret_w_out
        else:
            proj, beta, g = gdn_proj(h, gain, gdn_w_in, gdn_w_small[j], j, gdn_conv[j], gdn_a_log[j],
                                     gdn_dt_bias[j], seq=s)
            y = gdn_core(proj.reshape(b, s, GDN_MAIN), beta.reshape(b, s, hv), g.reshape(b, s, hv),
                         gdn_norm_gain[j])
            w_mix = gdn_w_out
        h = mix_ffn_residual(h, y.reshape(m, -1), w_mix, j, norm_ffn[i].reshape(1, d), ffn_w_in, ffn_w_out, i,
                             final_gain, final_norm=(i == DEPTH - 1))
    return h.reshape(b, s, d)
```

```python
import functools

import jax
import jax.numpy as jnp
from jax import lax
from jax.experimental import pallas as pl
from jax.experimental.pallas import tpu as pltpu

F32 = jnp.float32
BF16 = jnp.bfloat16
HIGHEST = lax.Precision.HIGHEST

D_MODEL = 1024
DEPTH = 4
N_MIXERS = 2

RET_HEADS = 4
RET_HEAD_QK = D_MODEL // RET_HEADS
RET_HEAD_V = 2 * RET_HEAD_QK
RET_QK = RET_HEADS * RET_HEAD_QK
RET_V = RET_HEADS * RET_HEAD_V
RET_IN = 2 * RET_QK + 2 * RET_V
RET_CHUNK = 128
ROPE_THETA = 10000.0

GDN_HEAD = 128
GDN_K_HEADS = D_MODEL // GDN_HEAD
GDN_V_HEADS = 2 * GDN_K_HEADS
GDN_QK = GDN_K_HEADS * GDN_HEAD
GDN_V = GDN_V_HEADS * GDN_HEAD
GDN_CONV_DIM = 2 * GDN_QK + GDN_V
GDN_MAIN = GDN_CONV_DIM + GDN_V
GDN_CONV_K = 4
GDN_CHUNK = 64
GDN_INV_BLOCK = 8
PREV_ROWS = 16
GDN_PROJ_GROUP = 256

FFN_HIDDEN = 2816
NORM_EPS = 1e-6

VMEM_LIMIT = 48 * 1024 * 1024
VMEM_LIMIT_FFN = 56 * 1024 * 1024


def _params(*sem, vmem=VMEM_LIMIT):
    return pltpu.CompilerParams(dimension_semantics=sem, vmem_limit_bytes=vmem)


def _mm(a, b):
    return jnp.dot(a, b, preferred_element_type=F32)


def _mm_nt(a, b):
    return lax.dot_general(a, b, (((1,), (1,)), ((), ())), preferred_element_type=F32)


def _mm_tn(a, b):
    return lax.dot_general(a, b, (((0,), (0,)), ((), ())), preferred_element_type=F32)


def _hdot(a, b):
    return jnp.dot(a, b, precision=HIGHEST, preferred_element_type=F32)


def _bf(xs):
    return [x.astype(BF16) for x in xs]


def _sigmoid(x):
    return 1.0 / (1.0 + jnp.exp(-x))


def _silu(x):
    half = 0.5 * x
    return half + half * jnp.tanh(half)


def _rmsnorm_rows(x, gain):
    return x * lax.rsqrt(jnp.mean(x * x, axis=-1, keepdims=True) + NORM_EPS) * gain


def _ret_proj_kernel(x_ref, g_ref, w_ref, cos_ref, sin_ref, o_ref, xn_ref):
    dk = RET_HEAD_QK
    half = dk // 2
    rotary_cols = 2 * RET_QK

    @pl.when(pl.program_id(1) == 0)
    def _():
        xn_ref[...] = _rmsnorm_rows(x_ref[...], g_ref[...]).astype(BF16)
        cos = cos_ref[...]
        sin = sin_ref[...]
        project = lambda head: _mm(xn_ref[...], w_ref[:, pl.ds(head * dk, dk)])
        t_next = project(0)
        for head in range(rotary_cols // dk):
            t = t_next
            t_next = project(head + 1) if (head + 1) * dk < rotary_cols else None
            t1, t2 = t[:, :half], t[:, half:]
            rot = jnp.concatenate([t1 * cos - t2 * sin, t2 * cos + t1 * sin], axis=-1)
            o_ref[:, pl.ds(head * dk, dk)] = rot.astype(o_ref.dtype)
        o_ref[:, rotary_cols:] = _mm(xn_ref[...], w_ref[:, rotary_cols:]).astype(o_ref.dtype)

    @pl.when(pl.program_id(1) != 0)
    def _():
        o_ref[...] = _mm(xn_ref[...], w_ref[...]).astype(o_ref.dtype)


def ret_proj(x, gain, w, layer, cos, sin, *, tm=1024, tn=3072):
    m, d = x.shape
    n = w.shape[2]
    assert tn >= 2 * RET_QK and n % tn == 0
    half = RET_HEAD_QK // 2
    return pl.pallas_call(
        _ret_proj_kernel, grid=(m // tm, n // tn),
        in_specs=[pl.BlockSpec((tm, d), lambda i, j: (i, 0)),
                  pl.BlockSpec((1, d), lambda i, j: (0, 0)),
                  pl.BlockSpec((None, d, tn), lambda i, j: (layer, 0, j)),
                  pl.BlockSpec((tm, half), lambda i, j: (i, 0)),
                  pl.BlockSpec((tm, half), lambda i, j: (i, 0))],
        out_specs=pl.BlockSpec((tm, tn), lambda i, j: (i, j)),
        out_shape=jax.ShapeDtypeStruct((m, n), BF16),
        scratch_shapes=[pltpu.VMEM((tm, d), BF16)],
        compiler_params=_params("parallel", "arbitrary"),
        name="ret_proj",
    )(x, gain, w, cos, sin)


def _rope_table_kernel(pos_ref, freq_ref, cos_ref, sin_ref):
    ang = pos_ref[...].astype(F32) * freq_ref[...]
    cos_ref[...] = jnp.cos(ang)
    sin_ref[...] = jnp.sin(ang)


def rope_tables(positions, *, ts=512):
    m = positions.size
    half = RET_HEAD_QK // 2
    inv_freq = (ROPE_THETA ** (-jnp.arange(half, dtype=F32) / half)).reshape(1, half)
    out = jax.ShapeDtypeStruct((m, half), F32)
    return pl.pallas_call(
        _rope_table_kernel, grid=(m // ts,),
        in_specs=[pl.BlockSpec((ts, 1), lambda i: (i, 0)),
                  pl.BlockSpec((1, half), lambda i: (0, 0))],
        out_specs=[pl.BlockSpec((ts, half), lambda i: (i, 0))] * 2,
        out_shape=[out, out], compiler_params=_params("parallel"),
        name="rope_tables",
    )(positions.reshape(m, 1), inv_freq)


def _retention_kernel(q_ref, k_ref, v_ref, gate_ref, decay_ref, qdec_ref,
                      kdec_ref, cdec_ref, gain_ref, y_ref, state_ref, *, chunks):
    c = RET_CHUNK
    dk, dv = RET_HEAD_QK, RET_HEAD_V
    heads = range(RET_HEADS)

    @pl.when(pl.program_id(1) == 0)
    def _():
        state_ref[...] = jnp.zeros_like(state_ref)

    qd, inner, upd = [], [], []
    for ci in range(chunks):
        rows = pl.ds(ci * c, c)
        for h in heads:
            q = q_ref[0, rows, pl.ds(h * dk, dk)]
            k = k_ref[0, rows, pl.ds(h * dk, dk)]
            v = v_ref[0, rows, pl.ds(h * dv, dv)]
            scores = _mm_nt(q, k) * decay_ref[h]
            inner.append(_mm(scores.astype(BF16), v))
            upd.append(_mm_tn((k.astype(F32) * kdec_ref[h]).astype(BF16), v))
            qd.append((q.astype(F32) * qdec_ref[h]).astype(BF16))

    for ci in range(chunks):
        rows = pl.ds(ci * c, c)
        for h in heads:
            p = ci * RET_HEADS + h
            state = state_ref[h]
            o = inner[p] + _mm(qd[p], state.astype(BF16))
            state_ref[h] = state * cdec_ref[h][0:1, 0:1] + upd[p]
            mu = jnp.mean(o, axis=-1, keepdims=True)
            var = jnp.mean(jnp.square(o - mu), axis=-1, keepdims=True)
            cols = pl.ds(h * dv, dv)
            on = (o - mu) * lax.rsqrt(var + NORM_EPS) * gain_ref[:, cols]
            y_ref[0, rows, cols] = (_silu(gate_ref[0, rows, cols].astype(F32)) * on).astype(y_ref.dtype)


def retention_core(proj, gn_gain, *, chunks=4):
    b, s, _ = proj.shape
    c = RET_CHUNK
    h = RET_HEADS
    rows = c * chunks
    log_gamma = jnp.log1p(-jnp.exp2(-5.0 - jnp.arange(h, dtype=F32)))
    pos = jnp.arange(c, dtype=F32)
    causal = pos[:, None] >= pos[None, :]
    diff = jnp.where(causal, pos[:, None] - pos[None, :], 0.0)
    k_scale = RET_HEAD_QK ** -0.5
    decay = jnp.where(causal, jnp.exp(log_gamma[:, None, None] * diff), 0.0) * k_scale
    qdec = jnp.exp(log_gamma[:, None] * (pos + 1.0))[:, :, None]
    kdec = jnp.exp(log_gamma[:, None] * (c - 1.0 - pos))[:, :, None] * k_scale
    cdec = jnp.broadcast_to(jnp.exp(log_gamma * c)[:, None, None], (h, 8, 128))
    dk, dv = RET_HEAD_QK, RET_HEAD_V
    tile = lambda w, blk: pl.BlockSpec((1, rows, w), lambda bi, ni: (bi, ni, blk))
    whole = lambda *shape: pl.BlockSpec(shape, lambda bi, ni: (0,) * len(shape))
    return pl.pallas_call(
        functools.partial(_retention_kernel, chunks=chunks),
        grid=(b, s // rows),
        in_specs=[
            tile(RET_QK, 0), tile(RET_QK, 1), tile(RET_V, 1), tile(RET_V, 2),
            whole(h, c, c), whole(h, c, 1), whole(h, c, 1), whole(h, 8, 128), whole(1, RET_V),
        ],
        out_specs=tile(RET_V, 0),
        out_shape=jax.ShapeDtypeStruct((b, s, RET_V), BF16),
        scratch_shapes=[pltpu.VMEM((h, dk, dv), F32)],
        compiler_params=_params("parallel", "arbitrary"),
        name="retention_core",
    )(proj, proj, proj, proj, decay, qdec, kdec, cdec, gn_gain.reshape(1, RET_V))


def _gdn_proj_kernel(x_ref, xp_ref, g_ref, w_ref, ws_ref, cw_ref, alog_ref, dtb_ref,
                     o_ref, beta_ref, gl_ref, xn_ref, ybuf_ref, *, tiles_per_seq):
    i = pl.program_id(0)
    j = pl.program_id(1)
    p = PREV_ROWS
    gw = GDN_PROJ_GROUP
    hv = GDN_V_HEADS

    @pl.when(j == 0)
    def _():
        first = (i % tiles_per_seq) == 0
        xp = _rmsnorm_rows(xp_ref[...], g_ref[...])
        xn_ref[0:p, :] = jnp.where(first, 0.0, xp).astype(BF16)
        xn = _rmsnorm_rows(x_ref[...], g_ref[...]).astype(BF16)
        xn_ref[p:, :] = xn
        ba = _mm(xn, ws_ref[...])
        beta_ref[...] = _sigmoid(ba[:, :hv])
        a = ba[:, hv:] + dtb_ref[...]
        softplus = jnp.maximum(a, 0.0) + jnp.log1p(jnp.exp(-jnp.abs(a)))
        gl_ref[...] = -jnp.exp(alog_ref[...]) * softplus

    groups = o_ref.shape[1] // gw
    tm = x_ref.shape[0]

    def project(group):
        return _mm(xn_ref[...], w_ref[:, pl.ds(group * gw, gw)])

    def conv_silu(y, group):
        cw = cw_ref[:, pl.ds(group * gw, gw)]
        ybuf = ybuf_ref.at[group % 2]
        ybuf[...] = y
        acc = y[p:] * cw[GDN_CONV_K - 1:GDN_CONV_K, :]
        for sh in range(1, GDN_CONV_K):
            acc = acc + ybuf[pl.ds(p - sh, tm), :] * cw[GDN_CONV_K - 1 - sh:GDN_CONV_K - sh, :]
        return _silu(acc)

    def l2n(y, scale):
        return y * (lax.rsqrt(jnp.sum(y * y, axis=-1, keepdims=True) + NORM_EPS) * scale)

    def conv_groups(finish):
        y_next = project(0)
        for group in range(groups):
            y = y_next
            if group + 1 < groups:
                y_next = project(group + 1)
            o_ref[:, pl.ds(group * gw, gw)] = finish(conv_silu(y, group), group).astype(o_ref.dtype)

    def qk_finish(s, group):
        scale = GDN_HEAD ** -0.5 if group * gw < GDN_QK else 1.0
        return jnp.concatenate([l2n(s[:, c0:c0 + GDN_HEAD], scale) for c0 in range(0, gw, GDN_HEAD)], axis=-1)

    @pl.when(j == 0)
    def _():
        conv_groups(qk_finish)

    @pl.when(j == 1)
    def _():
        conv_groups(lambda s, group: s)

    @pl.when(j == 2)
    def _():
        o_ref[...] = _mm(xn_ref[p:, :], w_ref[...]).astype(o_ref.dtype)


def gdn_proj(x, gain, w_in, w_small, layer, conv_w, a_log, dt_bias, *, seq, tm=1024):
    m, d = x.shape
    hv = GDN_V_HEADS
    tn = GDN_V
    assert GDN_CONV_DIM == 2 * tn and 2 * GDN_QK == tn and seq % tm == 0
    small = lambda shape: pl.BlockSpec(shape, lambda i, j: (0, 0))
    return pl.pallas_call(
        functools.partial(_gdn_proj_kernel, tiles_per_seq=seq // tm),
        grid=(m // tm, GDN_MAIN // tn),
        in_specs=[
            pl.BlockSpec((tm, d), lambda i, j: (i, 0)),
            pl.BlockSpec((PREV_ROWS, d), lambda i, j: (jnp.maximum(i * (tm // PREV_ROWS) - 1, 0), 0)),
            small((1, d)),
            pl.BlockSpec((None, d, tn), lambda i, j: (layer, 0, j)),
            small((d, 2 * hv)),
            pl.BlockSpec((GDN_CONV_K, tn), lambda i, j: (0, jnp.minimum(j, 1))),
            small((1, hv)), small((1, hv)),
        ],
        out_specs=[pl.BlockSpec((tm, tn), lambda i, j: (i, j)),
                   pl.BlockSpec((tm, hv), lambda i, j: (i, 0)),
                   pl.BlockSpec((tm, hv), lambda i, j: (i, 0))],
        out_shape=[jax.ShapeDtypeStruct((m, GDN_MAIN), BF16),
                   jax.ShapeDtypeStruct((m, hv), F32), jax.ShapeDtypeStruct((m, hv), F32)],
        scratch_shapes=[pltpu.VMEM((PREV_ROWS + tm, d), BF16),
                        pltpu.VMEM((2, PREV_ROWS + tm, GDN_PROJ_GROUP), F32)],
        compiler_params=_params("parallel", "arbitrary"),
        name="gdn_proj",
    )(x, x, gain, w_in, w_small, conv_w, a_log.reshape(1, hv), dt_bias.reshape(1, hv))


def _unit_lower_inverses(a_list, eye, base_mask, level_masks):
    d = [jnp.where(base_mask, a, 0.0) for a in a_list]
    db = _bf(d)
    d2b = _bf([_mm(x, x) for x in db])
    p = [eye - x for x in d]
    p = [pi + _mm(pi.astype(BF16), x2) for pi, x2 in zip(p, d2b)]
    d4b = _bf([_mm(x, x) for x in d2b])
    t = [pi + _mm(pi.astype(BF16), x4) for pi, x4 in zip(p, d4b)]
    for mask in level_masks:
        eb = _bf([jnp.where(mask, a, 0.0) for a in a_list])
        tb = _bf(t)
        teb = _bf([_mm(ti, ei) for ti, ei in zip(tb, eb)])
        t = [ti - _mm(tei, tbi) for ti, tei, tbi in zip(t, teb, tb)]
    return t


def _gdn_core_kernel(q_ref, k_ref, v_ref, beta_ref, g_ref, z_ref, gain_ref, y_ref, state_ref, *, chunks, sub):
    c = GDN_CHUNK
    dh = GDN_HEAD
    nk = GDN_K_HEADS
    nv = GDN_V_HEADS
    rep = nv // nk

    @pl.when(pl.program_id(1) == 0)
    def _():
        state_ref[...] = jnp.zeros_like(state_ref)

    ri = lax.broadcasted_iota(jnp.int32, (c, c), 0)
    ci = lax.broadcasted_iota(jnp.int32, (c, c), 1)
    causal = ri >= ci
    strict = ri > ci
    eye = (ri == ci).astype(F32)
    tril = causal.astype(F32)
    same = lambda size: (ri // size) == (ci // size)
    base_mask = same(GDN_INV_BLOCK)
    level_masks = []
    size = GDN_INV_BLOCK
    while size < c:
        level_masks.append(same(2 * size) & jnp.logical_not(same(size)))
        size *= 2
    gain = gain_ref[...]

    def group_step(idx, carry):
        rows = [pl.ds(pl.multiple_of((idx * sub + s_) * c, c), c) for s_ in range(sub)]
        heads = range(nv)
        vprobs = [(s_, h) for s_ in range(sub) for h in heads]
        kprobs = [(s_, h) for s_ in range(sub) for h in range(nk)]
        col = lambda x, h: x[:, h:h + 1]

        gcum = [_hdot(tril, g_ref[0, r, :]) for r in rows]
        beta = [beta_ref[0, r, :] for r in rows]
        glast = [x[c - 1:c, :] for x in gcum]
        eg = [jnp.exp(x) for x in gcum]
        kdec = [jnp.exp(gl - x) for gl, x in zip(glast, gcum)]
        eglast = [jnp.exp(gl) for gl in glast]

        q16 = {(s_, h): q_ref[0, rows[s_], pl.ds(h * dh, dh)] for s_, h in kprobs}
        k16 = {(s_, h): k_ref[0, rows[s_], pl.ds(h * dh, dh)] for s_, h in kprobs}
        q = {p: q16[p].astype(F32) for p in kprobs}
        k = {p: k16[p].astype(F32) for p in kprobs}
        qk_kk = {p: _mm_nt(jnp.concatenate([q16[p], k16[p]], axis=0), k16[p]) for p in kprobs}
        qk = {p: qk_kk[p][:c] for p in kprobs}
        kk = {p: qk_kk[p][c:] for p in kprobs}

        decay, a = {}, []
        for s_, h in vprobs:
            gcol = col(gcum[s_], h)
            grow = jnp.sum(gcol * eye, axis=0, keepdims=True)
            decay[s_, h] = jnp.where(causal, jnp.exp(jnp.where(causal, gcol - grow, 0.0)), 0.0)
            a.append(jnp.where(strict, kk[s_, h // rep] * col(beta[s_], h) * decay[s_, h], 0.0))
        t = _unit_lower_inverses(a, eye, base_mask, level_masks)

        rhs = []
        for s_, h in vprobs:
            b_col = col(beta[s_], h)
            v = v_ref[0, rows[s_], pl.ds(h * dh, dh)].astype(F32)
            rhs.append(jnp.concatenate([v * b_col, k[s_, h // rep] * (b_col * col(eg[s_], h))],
                                       axis=-1).astype(BF16))
        sol = dict(zip(vprobs, [_mm(ti, ri_) for ti, ri_ in zip(_bf(t), rhs)]))
        attn_kd = {(s_, h): jnp.concatenate(
            [jnp.where(causal, qk[s_, h // rep] * decay[s_, h], 0.0).astype(BF16),
             (k[s_, h // rep] * col(kdec[s_], h)).astype(BF16).T], axis=0) for s_, h in vprobs}
        wq = {(s_, h): jnp.concatenate([sol[s_, h][:, dh:], q[s_, h // rep] * col(eg[s_], h)],
                                       axis=0).astype(BF16) for s_, h in vprobs}

        state = [state_ref[h] for h in heads]
        for s_ in range(sub):
            ws_qs = [_mm(wq[s_, h], state[h].astype(BF16)) for h in heads]
            v_new = _bf([sol[s_, h][:, :dh] - ws_qs[h][:c] for h in heads])
            av_kv = [_mm(attn_kd[s_, h], v_new[h]) for h in heads]
            o = [ws_qs[h][c:] + av_kv[h][:c] for h in heads]
            state = [state[h] * eglast[s_][:, h:h + 1] + av_kv[h][c:] for h in heads]
            for h in heads:
                on = o[h] * lax.rsqrt(jnp.mean(o[h] * o[h], axis=-1, keepdims=True) + NORM_EPS) * gain
                cols = pl.ds(h * dh, dh)
                y_ref[0, rows[s_], cols] = (on * _silu(z_ref[0, rows[s_], cols].astype(F32))).astype(y_ref.dtype)
        for h in heads:
            state_ref[h] = state[h]
        return carry

    lax.fori_loop(0, chunks // sub, group_step, 0)


def gdn_core(proj, beta, g, norm_gain, *, chunks=8, sub=2):
    b, s, _ = proj.shape
    rows = GDN_CHUNK * chunks
    hv = GDN_V_HEADS
    tile = lambda w, blk=0: pl.BlockSpec((1, rows, w), lambda bi, ni: (bi, ni, blk))
    return pl.pallas_call(
        functools.partial(_gdn_core_kernel, chunks=chunks, sub=sub), grid=(b, s // rows),
        in_specs=[tile(GDN_QK, 0), tile(GDN_QK, 1), tile(GDN_V, 1), tile(hv), tile(hv),
                  tile(GDN_V, 2),
                  pl.BlockSpec((1, GDN_HEAD), lambda bi, ni: (0, 0))],
        out_specs=tile(GDN_V),
        out_shape=jax.ShapeDtypeStruct((b, s, GDN_V), BF16),
        scratch_shapes=[pltpu.VMEM((hv, GDN_HEAD, GDN_HEAD), F32)],
        compiler_params=_params("parallel", "arbitrary"),
        name="gdn_core",
    )(proj, proj, proj, beta, g, proj, norm_gain.reshape(1, GDN_HEAD))


def _mix_ffn_kernel(h_ref, y_ref, wmix_ref, g_ref, win_ref, wout_ref, fg_ref, o_ref, *, final_norm):
    h1 = h_ref[...] + _mm(y_ref[...], wmix_ref[...])
    xn = _rmsnorm_rows(h1, g_ref[...]).astype(BF16)
    gate = _mm(xn, win_ref[:, :FFN_HIDDEN])
    up = _mm(xn, win_ref[:, FFN_HIDDEN:])
    act = (_silu(gate) * up).astype(BF16)
    out = h1 + _mm(act, wout_ref[...])
    if final_norm:
        out = _rmsnorm_rows(out, fg_ref[...])
    o_ref[...] = out


def mix_ffn_residual(h, y, w_mix, mix_layer, gain, w_in, w_out, ffn_layer, final_gain, *, final_norm, tm=512):
    m, d = h.shape
    kdim = y.shape[1]
    resident = functools.partial(pl.BlockSpec, index_map=lambda i: (0, 0), pipeline_mode=pl.Buffered(1))
    layer_of = lambda layer, *shape: pl.BlockSpec((None,) + shape, lambda i: (layer, 0, 0),
                                                  pipeline_mode=pl.Buffered(1))
    rows = lambda w: pl.BlockSpec((tm, w), lambda i: (i, 0))
    return pl.pallas_call(
        functools.partial(_mix_ffn_kernel, final_norm=final_norm),
        grid=(m // tm,),
        in_specs=[rows(d), rows(kdim), layer_of(mix_layer, kdim, d), resident((1, d)),
                  layer_of(ffn_layer, d, 2 * FFN_HIDDEN), layer_of(ffn_layer, FFN_HIDDEN, d), resident((1, d))],
        out_specs=rows(d),
        out_shape=jax.ShapeDtypeStruct((m, d), F32),
        compiler_params=_params("parallel", vmem=VMEM_LIMIT_FFN),
        name="mix_ffn_residual",
    )(h, y, w_mix, gain, w_in, w_out, final_gain)


def kernel(x, positions, norm_mix, norm_ffn, norm_final, ret_w_in, ret_gn_gain, ret_w_out,
           gdn_w_in, gdn_conv, gdn_a_log, gdn_dt_bias, gdn_norm_gain, gdn_w_out,
           ffn_w_in, ffn_w_out):
    b, s, d = x.shape
    m = b * s
    h = x.reshape(m, d)
    cos, sin = rope_tables(positions)
    final_gain = norm_final.reshape(1, d)
    ret_w_in, ret_w_out, gdn_w_out, ffn_w_in, ffn_w_out = (
        w.astype(BF16) for w in (ret_w_in, ret_w_out, gdn_w_out, ffn_w_in, ffn_w_out))
    gdn_w_small = gdn_w_in[:, :, GDN_MAIN:].astype(BF16)
    gdn_w_in = gdn_w_in[:, :, :GDN_MAIN].astype(BF16)
    hv = GDN_V_HEADS
    for i in range(DEPTH):
        j = i // N_MIXERS
        gain = norm_mix[i].reshape(1, d)
        if i % N_MIXERS == 0:
            proj = ret_proj(h, gain, ret_w_in, j, cos, sin)
            y = retention_core(proj.reshape(b, s, RET_IN), ret_gn_gain[j])
            w_mix = ret_w_out
        else:
            proj, beta, g = gdn_proj(h, gain, gdn_w_in, gdn_w_small[j], j, gdn_conv[j], gdn_a_log[j],
                                     gdn_dt_bias[j], seq=s)
            y = gdn_core(proj.reshape(b, s, GDN_MAIN), beta.reshape(b, s, hv), g.reshape(b, s, hv),
                         gdn_norm_gain[j])
            w_mix = gdn_w_out
        h = mix_ffn_residual(h, y.reshape(m, -1), w_mix, j, norm_ffn[i].reshape(1, d), ffn_w_in, ffn_w_out, i,
                             final_gain, final_norm=(i == DEPTH - 1))
    return h.reshape(b, s, d)
```

```python
import functools

import jax
import jax.numpy as jnp
from jax import lax
from jax.experimental import pallas as pl
from jax.experimental.pallas import tpu as pltpu

F32 = jnp.float32
BF16 = jnp.bfloat16
HIGHEST = lax.Precision.HIGHEST

D_MODEL = 1024
DEPTH = 4
N_MIXERS = 2

RET_HEADS = 4
RET_HEAD_QK = D_MODEL // RET_HEADS
RET_HEAD_V = 2 * RET_HEAD_QK
RET_QK = RET_HEADS * RET_HEAD_QK
RET_V = RET_HEADS * RET_HEAD_V
RET_IN = 2 * RET_QK + 2 * RET_V
RET_CHUNK = 128
ROPE_THETA = 10000.0

GDN_HEAD = 128
GDN_K_HEADS = D_MODEL // GDN_HEAD
GDN_V_HEADS = 2 * GDN_K_HEADS
GDN_QK = GDN_K_HEADS * GDN_HEAD
GDN_V = GDN_V_HEADS * GDN_HEAD
GDN_CONV_DIM = 2 * GDN_QK + GDN_V
GDN_MAIN = GDN_CONV_DIM + GDN_V
GDN_CONV_K = 4
GDN_CHUNK = 64
GDN_INV_BLOCK = 8
PREV_ROWS = 16
GDN_PROJ_GROUP = 256
LANES = 128

FFN_HIDDEN = 2816
NORM_EPS = 1e-6

VMEM_LIMIT = 48 * 1024 * 1024


def _params(*sem):
    return pltpu.CompilerParams(dimension_semantics=sem, vmem_limit_bytes=VMEM_LIMIT)


def _mm(a, b):
    return jnp.dot(a, b, preferred_element_type=F32)


def _mm_nt(a, b):
    return lax.dot_general(a, b, (((1,), (1,)), ((), ())), preferred_element_type=F32)


def _mm_tn(a, b):
    return lax.dot_general(a, b, (((0,), (0,)), ((), ())), preferred_element_type=F32)


def _hdot(a, b):
    return jnp.dot(a, b, precision=HIGHEST, preferred_element_type=F32)


def _bf(xs):
    return [x.astype(BF16) for x in xs]


def _sigmoid(x):
    return 1.0 / (1.0 + jnp.exp(-x))


def _silu(x):
    half = 0.5 * x
    return half + half * jnp.tanh(half)


def _rmsnorm_rows(x, gain):
    return x * lax.rsqrt(jnp.mean(x * x, axis=-1, keepdims=True) + NORM_EPS) * gain


CAST_SLABS = 16


def _with_casts(body, n_in, n_out, n_cast, steps_per_slab):
    def kernel_fn(*refs):
        ins, refs = refs[:n_in], refs[n_in:]
        cast_in, refs = refs[:n_cast], refs[n_cast:]
        outs, refs = refs[:n_out], refs[n_out:]
        cast_out, scratch = refs[:n_cast], refs[n_cast:]
        step = pl.program_id(0) * pl.num_programs(1) + pl.program_id(1)

        @pl.when(step % steps_per_slab == 0)
        def _():
            for src, dst in zip(cast_in, cast_out):
                dst[...] = src[...].astype(dst.dtype)

        body(*ins, *outs, *scratch)
    return kernel_fn


def _cast_specs(casts, grid):
    steps_per_slab = grid[0] * grid[1] // CAST_SLABS
    assert steps_per_slab * CAST_SLABS == grid[0] * grid[1]
    slab_of = lambda i, j: (i * grid[1] + j) // steps_per_slab
    in_specs, out_specs, out_shapes = [], [], []
    for stack, layer, cols in casts:
        rows = stack.shape[1]
        slab = rows // CAST_SLABS
        assert slab * CAST_SLABS == rows and slab % 16 == 0 and cols % LANES == 0
        in_specs.append(pl.BlockSpec((None, slab, cols), lambda i, j, layer=layer: (layer, slab_of(i, j), 0)))
        out_specs.append(pl.BlockSpec((slab, cols), lambda i, j: (slab_of(i, j), 0)))
        out_shapes.append(jax.ShapeDtypeStruct((rows, cols), BF16))
    return in_specs, out_specs, out_shapes, steps_per_slab


def _ret_proj_kernel(x_ref, g_ref, w_ref, cos_ref, sin_ref, o_ref, xn_ref):
    dk = RET_HEAD_QK
    half = dk // 2
    rotary_cols = 2 * RET_QK

    @pl.when(pl.program_id(1) == 0)
    def _():
        xn_ref[...] = _rmsnorm_rows(x_ref[...], g_ref[...]).astype(BF16)
        cos = cos_ref[...]
        sin = sin_ref[...]
        project = lambda head: _mm(xn_ref[...], w_ref[:, pl.ds(head * dk, dk)])
        t_next = project(0)
        for head in range(rotary_cols // dk):
            t = t_next
            t_next = project(head + 1) if (head + 1) * dk < rotary_cols else None
            t1, t2 = t[:, :half], t[:, half:]
            rot = jnp.concatenate([t1 * cos - t2 * sin, t2 * cos + t1 * sin], axis=-1)
            o_ref[:, pl.ds(head * dk, dk)] = rot.astype(o_ref.dtype)
        o_ref[:, rotary_cols:] = _mm(xn_ref[...], w_ref[:, rotary_cols:]).astype(o_ref.dtype)

    @pl.when(pl.program_id(1) != 0)
    def _():
        o_ref[...] = _mm(xn_ref[...], w_ref[...]).astype(o_ref.dtype)


def ret_proj(x, gain, w, cos, sin, *, tm=1024, tn=3072):
    m, d = x.shape
    n = w.shape[1]
    assert tn >= 2 * RET_QK and n % tn == 0
    half = RET_HEAD_QK // 2
    return pl.pallas_call(
        _ret_proj_kernel, grid=(m // tm, n // tn),
        in_specs=[pl.BlockSpec((tm, d), lambda i, j: (i, 0)),
                  pl.BlockSpec((1, d), lambda i, j: (0, 0)),
                  pl.BlockSpec((d, tn), lambda i, j: (0, j)),
                  pl.BlockSpec((tm, half), lambda i, j: (i, 0)),
                  pl.BlockSpec((tm, half), lambda i, j: (i, 0))],
        out_specs=pl.BlockSpec((tm, tn), lambda i, j: (i, j)),
        out_shape=jax.ShapeDtypeStruct((m, n), BF16),
        scratch_shapes=[pltpu.VMEM((tm, d), BF16)],
        compiler_params=_params("parallel", "arbitrary"),
        name="ret_proj",
    )(x, gain, w, cos, sin)


def _rope_table_kernel(pos_ref, freq_ref, cos_ref, sin_ref):
    ang = pos_ref[...].astype(F32) * freq_ref[...]
    cos_ref[...] = jnp.cos(ang)
    sin_ref[...] = jnp.sin(ang)


def rope_tables(positions, *, ts=512):
    m = positions.size
    half = RET_HEAD_QK // 2
    inv_freq = (ROPE_THETA ** (-jnp.arange(half, dtype=F32) / half)).reshape(1, half)
    out = jax.ShapeDtypeStruct((m, half), F32)
    return pl.pallas_call(
        _rope_table_kernel, grid=(m // ts,),
        in_specs=[pl.BlockSpec((ts, 1), lambda i: (i, 0)),
                  pl.BlockSpec((1, half), lambda i: (0, 0))],
        out_specs=[pl.BlockSpec((ts, half), lambda i: (i, 0))] * 2,
        out_shape=[out, out], compiler_params=_params("parallel"),
        name="rope_tables",
    )(positions.reshape(m, 1), inv_freq)


def _retention_kernel(q_ref, k_ref, v_ref, gate_ref, decay_ref, qdec_ref,
                      kdec_ref, cdec_ref, gain_ref, y_ref, state_ref, *, chunks):
    c = RET_CHUNK
    dk, dv = RET_HEAD_QK, RET_HEAD_V
    heads = range(RET_HEADS)

    @pl.when(pl.program_id(1) == 0)
    def _():
        state_ref[...] = jnp.zeros_like(state_ref)

    qd, inner, upd = [], [], []
    for ci in range(chunks):
        rows = pl.ds(ci * c, c)
        for h in heads:
            q = q_ref[0, rows, pl.ds(h * dk, dk)]
            k = k_ref[0, rows, pl.ds(h * dk, dk)]
            v = v_ref[0, rows, pl.ds(h * dv, dv)]
            scores = _mm_nt(q, k) * decay_ref[h]
            inner.append(_mm(scores.astype(BF16), v))
            upd.append(_mm_tn((k.astype(F32) * kdec_ref[h]).astype(BF16), v))
            qd.append((q.astype(F32) * qdec_ref[h]).astype(BF16))

    for ci in range(chunks):
        rows = pl.ds(ci * c, c)
        for h in heads:
            p = ci * RET_HEADS + h
            state = state_ref[h]
            o = inner[p] + _mm(qd[p], state.astype(BF16))
            state_ref[h] = state * cdec_ref[h][0:1, 0:1] + upd[p]
            mu = jnp.mean(o, axis=-1, keepdims=True)
            var = jnp.mean(jnp.square(o - mu), axis=-1, keepdims=True)
            cols = pl.ds(h * dv, dv)
            on = (o - mu) * lax.rsqrt(var + NORM_EPS) * gain_ref[:, cols]
            y_ref[0, rows, cols] = (_silu(gate_ref[0, rows, cols].astype(F32)) * on).astype(y_ref.dtype)


def retention_core(proj, gn_gain, casts, *, chunks=4):
    b, s, _ = proj.shape
    c = RET_CHUNK
    h = RET_HEADS
    rows = c * chunks
    log_gamma = jnp.log1p(-jnp.exp2(-5.0 - jnp.arange(h, dtype=F32)))
    pos = jnp.arange(c, dtype=F32)
    causal = pos[:, None] >= pos[None, :]
    diff = jnp.where(causal, pos[:, None] - pos[None, :], 0.0)
    k_scale = RET_HEAD_QK ** -0.5
    decay = jnp.where(causal, jnp.exp(log_gamma[:, None, None] * diff), 0.0) * k_scale
    qdec = jnp.exp(log_gamma[:, None] * (pos + 1.0))[:, :, None]
    kdec = jnp.exp(log_gamma[:, None] * (c - 1.0 - pos))[:, :, None] * k_scale
    cdec = jnp.broadcast_to(jnp.exp(log_gamma * c)[:, None, None], (h, 8, 128))
    dk, dv = RET_HEAD_QK, RET_HEAD_V
    tile = lambda w, blk: pl.BlockSpec((1, rows, w), lambda bi, ni: (bi, ni, blk))
    whole = lambda *shape: pl.BlockSpec(shape, lambda bi, ni: (0,) * len(shape))
    grid = (b, s // rows)
    cast_in, cast_out, cast_shapes, steps_per_slab = _cast_specs(casts, grid)
    in_specs = [
        tile(RET_QK, 0), tile(RET_QK, 1), tile(RET_V, 1), tile(RET_V, 2),
        whole(h, c, c), whole(h, c, 1), whole(h, c, 1), whole(h, 8, 128), whole(1, RET_V),
    ]
    body = functools.partial(_retention_kernel, chunks=chunks)
    y, *cast = pl.pallas_call(
        _with_casts(body, len(in_specs), 1, len(casts), steps_per_slab), grid=grid,
        in_specs=in_specs + cast_in,
        out_specs=[tile(RET_V, 0)] + cast_out,
        out_shape=[jax.ShapeDtypeStruct((b, s, RET_V), BF16)] + cast_shapes,
        scratch_shapes=[pltpu.VMEM((h, dk, dv), F32)],
        compiler_params=_params("parallel", "arbitrary"),
        name="retention_core",
    )(proj, proj, proj, proj, decay, qdec, kdec, cdec, gn_gain.reshape(1, RET_V), *[w for w, _, _ in casts])
    return y, cast


def _gdn_proj_kernel(x_ref, xp_ref, g_ref, w_ref, ws_ref, cw_ref, alog_ref, dtb_ref,
                     o_ref, beta_ref, gl_ref, xn_ref, ybuf_ref, *, tiles_per_seq):
    i = pl.program_id(0)
    j = pl.program_id(1)
    p = PREV_ROWS
    gw = GDN_PROJ_GROUP
    hv = GDN_V_HEADS

    @pl.when(j == 0)
    def _():
        first = (i % tiles_per_seq) == 0
        xp = _rmsnorm_rows(xp_ref[...], g_ref[...])
        xn_ref[0:p, :] = jnp.where(first, 0.0, xp).astype(BF16)
        xn = _rmsnorm_rows(x_ref[...], g_ref[...]).astype(BF16)
        xn_ref[p:, :] = xn
        ba = _mm(xn, ws_ref[:, :2 * hv].astype(BF16))
        beta_ref[...] = _sigmoid(ba[:, :hv])
        a = ba[:, hv:] + dtb_ref[...]
        softplus = jnp.maximum(a, 0.0) + jnp.log1p(jnp.exp(-jnp.abs(a)))
        gl_ref[...] = -jnp.exp(alog_ref[...]) * softplus

    groups = o_ref.shape[1] // gw
    tm = x_ref.shape[0]

    def project(group):
        return _mm(xn_ref[...], w_ref[:, pl.ds(group * gw, gw)].astype(BF16))

    def conv_silu(y, group):
        cw = cw_ref[:, pl.ds(group * gw, gw)]
        ybuf = ybuf_ref.at[group % 2]
        ybuf[...] = y
        acc = y[p:] * cw[GDN_CONV_K - 1:GDN_CONV_K, :]
        for sh in range(1, GDN_CONV_K):
            acc = acc + ybuf[pl.ds(p - sh, tm), :] * cw[GDN_CONV_K - 1 - sh:GDN_CONV_K - sh, :]
        return _silu(acc)

    def l2n(y, scale):
        return y * (lax.rsqrt(jnp.sum(y * y, axis=-1, keepdims=True) + NORM_EPS) * scale)

    def conv_groups(finish):
        y_next = project(0)
        for group in range(groups):
            y = y_next
            if group + 1 < groups:
                y_next = project(group + 1)
            o_ref[:, pl.ds(group * gw, gw)] = finish(conv_silu(y, group), group).astype(o_ref.dtype)

    def qk_finish(s, group):
        scale = GDN_HEAD ** -0.5 if group * gw < GDN_QK else 1.0
        return jnp.concatenate([l2n(s[:, c0:c0 + GDN_HEAD], scale) for c0 in range(0, gw, GDN_HEAD)], axis=-1)

    @pl.when(j == 0)
    def _():
        conv_groups(qk_finish)

    @pl.when(j == 1)
    def _():
        conv_groups(lambda s, group: s)

    @pl.when(j == 2)
    def _():
        o_ref[...] = _mm(xn_ref[p:, :], w_ref[...].astype(BF16)).astype(o_ref.dtype)


def gdn_proj(x, gain, w_in, layer, conv_w, a_log, dt_bias, *, seq, tm=1024):
    m, d = x.shape
    hv = GDN_V_HEADS
    tn = GDN_V
    assert GDN_CONV_DIM == 2 * tn and 2 * GDN_QK == tn and seq % tm == 0
    small = lambda shape: pl.BlockSpec(shape, lambda i, j: (0, 0))
    return pl.pallas_call(
        functools.partial(_gdn_proj_kernel, tiles_per_seq=seq // tm),
        grid=(m // tm, GDN_MAIN // tn),
        in_specs=[
            pl.BlockSpec((tm, d), lambda i, j: (i, 0)),
            pl.BlockSpec((PREV_ROWS, d), lambda i, j: (jnp.maximum(i * (tm // PREV_ROWS) - 1, 0), 0)),
            small((1, d)),
            pl.BlockSpec((None, d, tn), lambda i, j: (layer, 0, j)),
            pl.BlockSpec((None, d, LANES), lambda i, j: (layer, 0, GDN_MAIN // LANES)),
            pl.BlockSpec((GDN_CONV_K, tn), lambda i, j: (0, jnp.minimum(j, 1))),
            small((1, hv)), small((1, hv)),
        ],
        out_specs=[pl.BlockSpec((tm, tn), lambda i, j: (i, j)),
                   pl.BlockSpec((tm, hv), lambda i, j: (i, 0)),
                   pl.BlockSpec((tm, hv), lambda i, j: (i, 0))],
        out_shape=[jax.ShapeDtypeStruct((m, GDN_MAIN), BF16),
                   jax.ShapeDtypeStruct((m, hv), F32), jax.ShapeDtypeStruct((m, hv), F32)],
        scratch_shapes=[pltpu.VMEM((PREV_ROWS + tm, d), BF16),
                        pltpu.VMEM((2, PREV_ROWS + tm, GDN_PROJ_GROUP), F32)],
        compiler_params=_params("parallel", "arbitrary"),
        name="gdn_proj",
    )(x, x, gain, w_in, w_in, conv_w, a_log.reshape(1, hv), dt_bias.reshape(1, hv))


def _unit_lower_inverses(a_list, eye, base_mask, level_masks):
    d = [jnp.where(base_mask, a, 0.0) for a in a_list]
    db = _bf(d)
    d2b = _bf([_mm(x, x) for x in db])
    p = [eye - x for x in d]
    p = [pi + _mm(pi.astype(BF16), x2) for pi, x2 in zip(p, d2b)]
    d4b = _bf([_mm(x, x) for x in d2b])
    t = [pi + _mm(pi.astype(BF16), x4) for pi, x4 in zip(p, d4b)]
    for mask in level_masks:
        eb = _bf([jnp.where(mask, a, 0.0) for a in a_list])
        tb = _bf(t)
        teb = _bf([_mm(ti, ei) for ti, ei in zip(tb, eb)])
        t = [ti - _mm(tei, tbi) for ti, tei, tbi in zip(t, teb, tb)]
    return t


def _gdn_core_kernel(q_ref, k_ref, v_ref, beta_ref, g_ref, z_ref, gain_ref, y_ref, state_ref, *, chunks, sub):
    c = GDN_CHUNK
    dh = GDN_HEAD
    nk = GDN_K_HEADS
    nv = GDN_V_HEADS
    rep = nv // nk

    @pl.when(pl.program_id(1) == 0)
    def _():
        state_ref[...] = jnp.zeros_like(state_ref)

    ri = lax.broadcasted_iota(jnp.int32, (c, c), 0)
    ci = lax.broadcasted_iota(jnp.int32, (c, c), 1)
    causal = ri >= ci
    strict = ri > ci
    eye = (ri == ci).astype(F32)
    tril = causal.astype(F32)
    same = lambda size: (ri // size) == (ci // size)
    base_mask = same(GDN_INV_BLOCK)
    level_masks = []
    size = GDN_INV_BLOCK
    while size < c:
        level_masks.append(same(2 * size) & jnp.logical_not(same(size)))
        size *= 2
    gain = gain_ref[...]

    def group_step(idx, carry):
        rows = [pl.ds(pl.multiple_of((idx * sub + s_) * c, c), c) for s_ in range(sub)]
        heads = range(nv)
        vprobs = [(s_, h) for s_ in range(sub) for h in heads]
        kprobs = [(s_, h) for s_ in range(sub) for h in range(nk)]
        col = lambda x, h: x[:, h:h + 1]

        gcum = [_hdot(tril, g_ref[0, r, :]) for r in rows]
        beta = [beta_ref[0, r, :] for r in rows]
        glast = [x[c - 1:c, :] for x in gcum]
        eg = [jnp.exp(x) for x in gcum]
        kdec = [jnp.exp(gl - x) for gl, x in zip(glast, gcum)]
        eglast = [jnp.exp(gl) for gl in glast]

        q16 = {(s_, h): q_ref[0, rows[s_], pl.ds(h * dh, dh)] for s_, h in kprobs}
        k16 = {(s_, h): k_ref[0, rows[s_], pl.ds(h * dh, dh)] for s_, h in kprobs}
        q = {p: q16[p].astype(F32) for p in kprobs}
        k = {p: k16[p].astype(F32) for p in kprobs}
        qk_kk = {p: _mm_nt(jnp.concatenate([q16[p], k16[p]], axis=0), k16[p]) for p in kprobs}
        qk = {p: qk_kk[p][:c] for p in kprobs}
        kk = {p: qk_kk[p][c:] for p in kprobs}

        decay, a = {}, []
        for s_, h in vprobs:
            gcol = col(gcum[s_], h)
            grow = jnp.sum(gcol * eye, axis=0, keepdims=True)
            decay[s_, h] = jnp.where(causal, jnp.exp(jnp.where(causal, gcol - grow, 0.0)), 0.0)
            a.append(jnp.where(strict, kk[s_, h // rep] * col(beta[s_], h) * decay[s_, h], 0.0))
        t = _unit_lower_inverses(a, eye, base_mask, level_masks)

        rhs = []
        for s_, h in vprobs:
            b_col = col(beta[s_], h)
            v = v_ref[0, rows[s_], pl.ds(h * dh, dh)].astype(F32)
            rhs.append(jnp.concatenate([v * b_col, k[s_, h // rep] * (b_col * col(eg[s_], h))],
                                       axis=-1).astype(BF16))
        sol = dict(zip(vprobs, [_mm(ti, ri_) for ti, ri_ in zip(_bf(t), rhs)]))
        attn_kd = {(s_, h): jnp.concatenate(
            [jnp.where(causal, qk[s_, h // rep] * decay[s_, h], 0.0).astype(BF16),
             (k[s_, h // rep] * col(kdec[s_], h)).astype(BF16).T], axis=0) for s_, h in vprobs}
        wq = {(s_, h): jnp.concatenate([sol[s_, h][:, dh:], q[s_, h // rep] * col(eg[s_], h)],
                                       axis=0).astype(BF16) for s_, h in vprobs}

        state = [state_ref[h] for h in heads]
        for s_ in range(sub):
            ws_qs = [_mm(wq[s_, h], state[h].astype(BF16)) for h in heads]
            v_new = _bf([sol[s_, h][:, :dh] - ws_qs[h][:c] for h in heads])
            av_kv = [_mm(attn_kd[s_, h], v_new[h]) for h in heads]
            o = [ws_qs[h][c:] + av_kv[h][:c] for h in heads]
            state = [state[h] * eglast[s_][:, h:h + 1] + av_kv[h][c:] for h in heads]
            for h in heads:
                on = o[h] * lax.rsqrt(jnp.mean(o[h] * o[h], axis=-1, keepdims=True) + NORM_EPS) * gain
                cols = pl.ds(h * dh, dh)
                y_ref[0, rows[s_], cols] = (on * _silu(z_ref[0, rows[s_], cols].astype(F32))).astype(y_ref.dtype)
        for h in heads:
            state_ref[h] = state[h]
        return carry

    lax.fori_loop(0, chunks // sub, group_step, 0)


def gdn_core(proj, beta, g, norm_gain, casts, *, chunks=8, sub=2):
    b, s, _ = proj.shape
    rows = GDN_CHUNK * chunks
    hv = GDN_V_HEADS
    tile = lambda w, blk=0: pl.BlockSpec((1, rows, w), lambda bi, ni: (bi, ni, blk))
    grid = (b, s // rows)
    cast_in, cast_out, cast_shapes, steps_per_slab = _cast_specs(casts, grid)
    in_specs = [tile(GDN_QK, 0), tile(GDN_QK, 1), tile(GDN_V, 1), tile(hv), tile(hv), tile(GDN_V, 2),
                pl.BlockSpec((1, GDN_HEAD), lambda bi, ni: (0, 0))]
    body = functools.partial(_gdn_core_kernel, chunks=chunks, sub=sub)
    y, *cast = pl.pallas_call(
        _with_casts(body, len(in_specs), 1, len(casts), steps_per_slab), grid=grid,
        in_specs=in_specs + cast_in,
        out_specs=[tile(GDN_V)] + cast_out,
        out_shape=[jax.ShapeDtypeStruct((b, s, GDN_V), BF16)] + cast_shapes,
        scratch_shapes=[pltpu.VMEM((hv, GDN_HEAD, GDN_HEAD), F32)],
        compiler_params=_params("parallel", "arbitrary"),
        name="gdn_core",
    )(proj, proj, proj, beta, g, proj, norm_gain.reshape(1, GDN_HEAD), *[w for w, _, _ in casts])
    return y, cast


def _mix_ffn_kernel(h_ref, y_ref, wmix_ref, g_ref, win_ref, wout_ref, fg_ref, o_ref, *, final_norm):
    h1 = h_ref[...] + _mm(y_ref[...], wmix_ref[...])
    xn = _rmsnorm_rows(h1, g_ref[...]).astype(BF16)
    gate = _mm(xn, win_ref[:, :FFN_HIDDEN])
    up = _mm(xn, win_ref[:, FFN_HIDDEN:])
    act = (_silu(gate) * up).astype(BF16)
    out = h1 + _mm(act, wout_ref[...])
    if final_norm:
        out = _rmsnorm_rows(out, fg_ref[...])
    o_ref[...] = out


def mix_ffn_residual(h, y, w_mix, gain, w_in, w_out, final_gain, *, final_norm, tm=512):
    m, d = h.shape
    kdim = y.shape[1]
    resident = functools.partial(pl.BlockSpec, index_map=lambda i: (0, 0), pipeline_mode=pl.Buffered(1))
    rows = lambda w: pl.BlockSpec((tm, w), lambda i: (i, 0))
    return pl.pallas_call(
        functools.partial(_mix_ffn_kernel, final_norm=final_norm),
        grid=(m // tm,),
        in_specs=[rows(d), rows(kdim), resident((kdim, d)), resident((1, d)),
                  resident((d, 2 * FFN_HIDDEN)), resident((FFN_HIDDEN, d)), resident((1, d))],
        out_specs=rows(d),
        out_shape=jax.ShapeDtypeStruct((m, d), F32),
        compiler_params=_params("parallel"),
        name="mix_ffn_residual",
    )(h, y, w_mix, gain, w_in, w_out, final_gain)


def kernel(x, positions, norm_mix, norm_ffn, norm_final, ret_w_in, ret_gn_gain, ret_w_out,
           gdn_w_in, gdn_conv, gdn_a_log, gdn_dt_bias, gdn_norm_gain, gdn_w_out,
           ffn_w_in, ffn_w_out):
    b, s, d = x.shape
    m = b * s
    h = x.reshape(m, d)
    cos, sin = rope_tables(positions)
    final_gain = norm_final.reshape(1, d)
    proj_w = ret_w_in[0].astype(BF16)
    hv = GDN_V_HEADS
    for i in range(DEPTH):
        j = i // N_MIXERS
        retention = i % N_MIXERS == 0
        gain = norm_mix[i].reshape(1, d)
        casts = [(ret_w_out if retention else gdn_w_out, j, d), (ffn_w_in, i, 2 * FFN_HIDDEN), (ffn_w_out, i, d)]
        if retention:
            proj = ret_proj(h, gain, proj_w, cos, sin)
            y, cast = retention_core(proj.reshape(b, s, RET_IN), ret_gn_gain[j], casts)
        else:
            if i + 1 < DEPTH:
                casts.append((ret_w_in, j + 1, RET_IN))
            proj, beta, g = gdn_proj(h, gain, gdn_w_in, j, gdn_conv[j], gdn_a_log[j], gdn_dt_bias[j], seq=s)
            y, cast = gdn_core(proj.reshape(b, s, GDN_MAIN), beta.reshape(b, s, hv), g.reshape(b, s, hv),
                               gdn_norm_gain[j], casts)
            proj_w = cast[3] if i + 1 < DEPTH else None
        w_mix, w_ffn_in, w_ffn_out = cast[:3]
        h = mix_ffn_residual(h, y.reshape(m, -1), w_mix, norm_ffn[i].reshape(1, d), w_ffn_in, w_ffn_out,
                             final_gain, final_norm=(i == DEPTH - 1))
    return h.reshape(b, s, d)
```

```python
import functools

import jax
import jax.numpy as jnp
from jax import lax
from jax.experimental import pallas as pl
from jax.experimental.pallas import tpu as pltpu

F32 = jnp.float32
BF16 = jnp.bfloat16
HIGHEST = lax.Precision.HIGHEST

D_MODEL = 1024
DEPTH = 4
N_MIXERS = 2

RET_HEADS = 4
RET_HEAD_QK = D_MODEL // RET_HEADS
RET_HEAD_V = 2 * RET_HEAD_QK
RET_QK = RET_HEADS * RET_HEAD_QK
RET_V = RET_HEADS * RET_HEAD_V
RET_IN = 2 * RET_QK + 2 * RET_V
RET_CHUNK = 128
ROPE_THETA = 10000.0

GDN_HEAD = 128
GDN_K_HEADS = D_MODEL // GDN_HEAD
GDN_V_HEADS = 2 * GDN_K_HEADS
GDN_QK = GDN_K_HEADS * GDN_HEAD
GDN_V = GDN_V_HEADS * GDN_HEAD
GDN_CONV_DIM = 2 * GDN_QK + GDN_V
GDN_MAIN = GDN_CONV_DIM + GDN_V
GDN_CONV_K = 4
GDN_CHUNK = 64
GDN_INV_BLOCK = 8
PREV_ROWS = 16
GDN_PROJ_GROUP = 256
LANES = 128

FFN_HIDDEN = 2816
NORM_EPS = 1e-6

VMEM_LIMIT = 48 * 1024 * 1024


def _params(*sem):
    return pltpu.CompilerParams(dimension_semantics=sem, vmem_limit_bytes=VMEM_LIMIT)


def _mm(a, b):
    return jnp.dot(a, b, preferred_element_type=F32)


def _mm_nt(a, b):
    return lax.dot_general(a, b, (((1,), (1,)), ((), ())), preferred_element_type=F32)


def _mm_tn(a, b):
    return lax.dot_general(a, b, (((0,), (0,)), ((), ())), preferred_element_type=F32)


def _hdot(a, b):
    return jnp.dot(a, b, precision=HIGHEST, preferred_element_type=F32)


def _bf(xs):
    return [x.astype(BF16) for x in xs]


def _sigmoid(x):
    return 1.0 / (1.0 + jnp.exp(-x))


def _silu(x):
    half = 0.5 * x
    return half + half * jnp.tanh(half)


def _rmsnorm_rows(x, gain):
    return x * lax.rsqrt(jnp.mean(x * x, axis=-1, keepdims=True) + NORM_EPS) * gain


CAST_SLABS = 16


def _with_casts(body, n_in, n_out, n_cast, steps_per_slab):
    def kernel_fn(*refs):
        ins, refs = refs[:n_in], refs[n_in:]
        cast_in, refs = refs[:n_cast], refs[n_cast:]
        outs, refs = refs[:n_out], refs[n_out:]
        cast_out, scratch = refs[:n_cast], refs[n_cast:]
        step = pl.program_id(0) * pl.num_programs(1) + pl.program_id(1)

        @pl.when(step % steps_per_slab == 0)
        def _():
            for src, dst in zip(cast_in, cast_out):
                dst[...] = src[...].astype(dst.dtype)

        body(*ins, *outs, *scratch)
    return kernel_fn


def _cast_specs(casts, grid):
    steps_per_slab = grid[0] * grid[1] // CAST_SLABS
    assert steps_per_slab * CAST_SLABS == grid[0] * grid[1]
    slab_of = lambda i, j: (i * grid[1] + j) // steps_per_slab
    in_specs, out_specs, out_shapes = [], [], []
    for stack, layer, cols in casts:
        rows = stack.shape[1]
        slab = rows // CAST_SLABS
        assert slab * CAST_SLABS == rows and slab % 16 == 0 and cols % LANES == 0
        in_specs.append(pl.BlockSpec((None, slab, cols), lambda i, j, layer=layer: (layer, slab_of(i, j), 0)))
        out_specs.append(pl.BlockSpec((slab, cols), lambda i, j: (slab_of(i, j), 0)))
        out_shapes.append(jax.ShapeDtypeStruct((rows, cols), BF16))
    return in_specs, out_specs, out_shapes, steps_per_slab


def _ret_proj_kernel(x_ref, g_ref, w_ref, cos_ref, sin_ref, o_ref, xn_ref):
    dk = RET_HEAD_QK
    half = dk // 2
    rotary_cols = 2 * RET_QK

    @pl.when(pl.program_id(1) == 0)
    def _():
        xn_ref[...] = _rmsnorm_rows(x_ref[...], g_ref[...]).astype(BF16)
        cos = cos_ref[...]
        sin = sin_ref[...]
        project = lambda head: _mm(xn_ref[...], w_ref[:, pl.ds(head * dk, dk)])
        t_next = project(0)
        for head in range(rotary_cols // dk):
            t = t_next
            t_next = project(head + 1) if (head + 1) * dk < rotary_cols else None
            t1, t2 = t[:, :half], t[:, half:]
            rot = jnp.concatenate([t1 * cos - t2 * sin, t2 * cos + t1 * sin], axis=-1)
            o_ref[:, pl.ds(head * dk, dk)] = rot.astype(o_ref.dtype)
        o_ref[:, rotary_cols:] = _mm(xn_ref[...], w_ref[:, rotary_cols:]).astype(o_ref.dtype)

    @pl.when(pl.program_id(1) != 0)
    def _():
        o_ref[...] = _mm(xn_ref[...], w_ref[...]).astype(o_ref.dtype)


def ret_proj(x, gain, w, cos, sin, *, tm=1024, tn=3072):
    m, d = x.shape
    n = w.shape[1]
    assert tn >= 2 * RET_QK and n % tn == 0
    half = RET_HEAD_QK // 2
    return pl.pallas_call(
        _ret_proj_kernel, grid=(m // tm, n // tn),
        in_specs=[pl.BlockSpec((tm, d), lambda i, j: (i, 0)),
                  pl.BlockSpec((1, d), lambda i, j: (0, 0)),
                  pl.BlockSpec((d, tn), lambda i, j: (0, j)),
                  pl.BlockSpec((tm, half), lambda i, j: (i, 0)),
                  pl.BlockSpec((tm, half), lambda i, j: (i, 0))],
        out_specs=pl.BlockSpec((tm, tn), lambda i, j: (i, j)),
        out_shape=jax.ShapeDtypeStruct((m, n), BF16),
        scratch_shapes=[pltpu.VMEM((tm, d), BF16)],
        compiler_params=_params("parallel", "arbitrary"),
        name="ret_proj",
    )(x, gain, w, cos, sin)


def _rope_table_kernel(pos_ref, freq_ref, cos_ref, sin_ref):
    ang = pos_ref[...].astype(F32) * freq_ref[...]
    cos_ref[...] = jnp.cos(ang)
    sin_ref[...] = jnp.sin(ang)


def rope_tables(positions, *, ts=512):
    m = positions.size
    half = RET_HEAD_QK // 2
    inv_freq = (ROPE_THETA ** (-jnp.arange(half, dtype=F32) / half)).reshape(1, half)
    out = jax.ShapeDtypeStruct((m, half), F32)
    return pl.pallas_call(
        _rope_table_kernel, grid=(m // ts,),
        in_specs=[pl.BlockSpec((ts, 1), lambda i: (i, 0)),
                  pl.BlockSpec((1, half), lambda i: (0, 0))],
        out_specs=[pl.BlockSpec((ts, half), lambda i: (i, 0))] * 2,
        out_shape=[out, out], compiler_params=_params("parallel"),
        name="rope_tables",
    )(positions.reshape(m, 1), inv_freq)


def _retention_kernel(q_ref, k_ref, v_ref, gate_ref, decay_ref, qdec_ref,
                      kdec_ref, cdec_ref, gain_ref, y_ref, state_ref, *, chunks):
    c = RET_CHUNK
    dk, dv = RET_HEAD_QK, RET_HEAD_V
    heads = range(RET_HEADS)

    @pl.when(pl.program_id(1) == 0)
    def _():
        state_ref[...] = jnp.zeros_like(state_ref)

    qd, inner, upd = [], [], []
    for ci in range(chunks):
        rows = pl.ds(ci * c, c)
        for h in heads:
            q = q_ref[0, rows, pl.ds(h * dk, dk)]
            k = k_ref[0, rows, pl.ds(h * dk, dk)]
            v = v_ref[0, rows, pl.ds(h * dv, dv)]
            scores = _mm_nt(q, k) * decay_ref[h]
            inner.append(_mm(scores.astype(BF16), v))
            upd.append(_mm_tn((k.astype(F32) * kdec_ref[h]).astype(BF16), v))
            qd.append((q.astype(F32) * qdec_ref[h]).astype(BF16))

    for ci in range(chunks):
        rows = pl.ds(ci * c, c)
        for h in heads:
            p = ci * RET_HEADS + h
            state = state_ref[h]
            o = inner[p] + _mm(qd[p], state.astype(BF16))
            state_ref[h] = state * cdec_ref[h][0:1, 0:1] + upd[p]
            mu = jnp.mean(o, axis=-1, keepdims=True)
            var = jnp.mean(jnp.square(o - mu), axis=-1, keepdims=True)
            cols = pl.ds(h * dv, dv)
            on = (o - mu) * lax.rsqrt(var + NORM_EPS) * gain_ref[:, cols]
            y_ref[0, rows, cols] = (_silu(gate_ref[0, rows, cols].astype(F32)) * on).astype(y_ref.dtype)


def retention_core(proj, gn_gain, casts, *, chunks=4):
    b, s, _ = proj.shape
    c = RET_CHUNK
    h = RET_HEADS
    rows = c * chunks
    log_gamma = jnp.log1p(-jnp.exp2(-5.0 - jnp.arange(h, dtype=F32)))
    pos = jnp.arange(c, dtype=F32)
    causal = pos[:, None] >= pos[None, :]
    diff = jnp.where(causal, pos[:, None] - pos[None, :], 0.0)
    k_scale = RET_HEAD_QK ** -0.5
    decay = jnp.where(causal, jnp.exp(log_gamma[:, None, None] * diff), 0.0) * k_scale
    qdec = jnp.exp(log_gamma[:, None] * (pos + 1.0))[:, :, None]
    kdec = jnp.exp(log_gamma[:, None] * (c - 1.0 - pos))[:, :, None] * k_scale
    cdec = jnp.broadcast_to(jnp.exp(log_gamma * c)[:, None, None], (h, 8, 128))
    dk, dv = RET_HEAD_QK, RET_HEAD_V
    tile = lambda w, blk: pl.BlockSpec((1, rows, w), lambda bi, ni: (bi, ni, blk))
    whole = lambda *shape: pl.BlockSpec(shape, lambda bi, ni: (0,) * len(shape))
    grid = (b, s // rows)
    cast_in, cast_out, cast_shapes, steps_per_slab = _cast_specs(casts, grid)
    in_specs = [
        tile(RET_QK, 0), tile(RET_QK, 1), tile(RET_V, 1), tile(RET_V, 2),
        whole(h, c, c), whole(h, c, 1), whole(h, c, 1), whole(h, 8, 128), whole(1, RET_V),
    ]
    body = functools.partial(_retention_kernel, chunks=chunks)
    y, *cast = pl.pallas_call(
        _with_casts(body, len(in_specs), 1, len(casts), steps_per_slab), grid=grid,
        in_specs=in_specs + cast_in,
        out_specs=[tile(RET_V, 0)] + cast_out,
        out_shape=[jax.ShapeDtypeStruct((b, s, RET_V), BF16)] + cast_shapes,
        scratch_shapes=[pltpu.VMEM((h, dk, dv), F32)],
        compiler_params=_params("parallel", "arbitrary"),
        name="retention_core",
    )(proj, proj, proj, proj, decay, qdec, kdec, cdec, gn_gain.reshape(1, RET_V), *[w for w, _, _ in casts])
    return y, cast


def _gdn_proj_kernel(x_ref, xp_ref, g_ref, w_ref, ws_ref, cw_ref, alog_ref, dtb_ref,
                     o_ref, beta_ref, gl_ref, xn_ref, ybuf_ref, *, tiles_per_seq):
    i = pl.program_id(0)
    j = pl.program_id(1)
    p = PREV_ROWS
    gw = GDN_PROJ_GROUP
    hv = GDN_V_HEADS

    @pl.when(j == 0)
    def _():
        first = (i % tiles_per_seq) == 0
        xp = _rmsnorm_rows(xp_ref[...], g_ref[...])
        xn_ref[0:p, :] = jnp.where(first, 0.0, xp).astype(BF16)
        xn = _rmsnorm_rows(x_ref[...], g_ref[...]).astype(BF16)
        xn_ref[p:, :] = xn
        ba = _mm(xn, ws_ref[:, :2 * hv].astype(BF16))
        beta_ref[...] = _sigmoid(ba[:, :hv])
        a = ba[:, hv:] + dtb_ref[...]
        softplus = jnp.maximum(a, 0.0) + jnp.log1p(jnp.exp(-jnp.abs(a)))
        gl_ref[...] = -jnp.exp(alog_ref[...]) * softplus

    groups = o_ref.shape[1] // gw
    tm = x_ref.shape[0]

    def project(group):
        return _mm(xn_ref[...], w_ref[:, pl.ds(group * gw, gw)])

    def conv_silu(y, group):
        cw = cw_ref[:, pl.ds(group * gw, gw)]
        ybuf = ybuf_ref.at[group % 2]
        ybuf[...] = y
        acc = y[p:] * cw[GDN_CONV_K - 1:GDN_CONV_K, :]
        for sh in range(1, GDN_CONV_K):
            acc = acc + ybuf[pl.ds(p - sh, tm), :] * cw[GDN_CONV_K - 1 - sh:GDN_CONV_K - sh, :]
        return _silu(acc)

    def l2n(y, scale):
        return y * (lax.rsqrt(jnp.sum(y * y, axis=-1, keepdims=True) + NORM_EPS) * scale)

    def conv_groups(finish):
        y_next = project(0)
        for group in range(groups):
            y = y_next
            if group + 1 < groups:
                y_next = project(group + 1)
            o_ref[:, pl.ds(group * gw, gw)] = finish(conv_silu(y, group), group).astype(o_ref.dtype)

    def qk_finish(s, group):
        scale = GDN_HEAD ** -0.5 if group * gw < GDN_QK else 1.0
        return jnp.concatenate([l2n(s[:, c0:c0 + GDN_HEAD], scale) for c0 in range(0, gw, GDN_HEAD)], axis=-1)

    @pl.when(j == 0)
    def _():
        conv_groups(qk_finish)

    @pl.when(j == 1)
    def _():
        conv_groups(lambda s, group: s)

    @pl.when(j == 2)
    def _():
        o_ref[...] = _mm(xn_ref[p:, :], w_ref[...]).astype(o_ref.dtype)


def gdn_proj(x, gain, w_main, w_in, layer, conv_w, a_log, dt_bias, *, seq, tm=1024):
    m, d = x.shape
    hv = GDN_V_HEADS
    tn = GDN_V
    assert GDN_CONV_DIM == 2 * tn and 2 * GDN_QK == tn and seq % tm == 0
    small = lambda shape: pl.BlockSpec(shape, lambda i, j: (0, 0))
    return pl.pallas_call(
        functools.partial(_gdn_proj_kernel, tiles_per_seq=seq // tm),
        grid=(m // tm, GDN_MAIN // tn),
        in_specs=[
            pl.BlockSpec((tm, d), lambda i, j: (i, 0)),
            pl.BlockSpec((PREV_ROWS, d), lambda i, j: (jnp.maximum(i * (tm // PREV_ROWS) - 1, 0), 0)),
            small((1, d)),
            pl.BlockSpec((d, tn), lambda i, j: (0, j)),
            pl.BlockSpec((None, d, LANES), lambda i, j: (layer, 0, GDN_MAIN // LANES)),
            pl.BlockSpec((GDN_CONV_K, tn), lambda i, j: (0, jnp.minimum(j, 1))),
            small((1, hv)), small((1, hv)),
        ],
        out_specs=[pl.BlockSpec((tm, tn), lambda i, j: (i, j)),
                   pl.BlockSpec((tm, hv), lambda i, j: (i, 0)),
                   pl.BlockSpec((tm, hv), lambda i, j: (i, 0))],
        out_shape=[jax.ShapeDtypeStruct((m, GDN_MAIN), BF16),
                   jax.ShapeDtypeStruct((m, hv), F32), jax.ShapeDtypeStruct((m, hv), F32)],
        scratch_shapes=[pltpu.VMEM((PREV_ROWS + tm, d), BF16),
                        pltpu.VMEM((2, PREV_ROWS + tm, GDN_PROJ_GROUP), F32)],
        compiler_params=_params("parallel", "arbitrary"),
        name="gdn_proj",
    )(x, x, gain, w_main, w_in, conv_w, a_log.reshape(1, hv), dt_bias.reshape(1, hv))


def _unit_lower_inverses(a_list, eye, base_mask, level_masks):
    d = [jnp.where(base_mask, a, 0.0) for a in a_list]
    db = _bf(d)
    d2b = _bf([_mm(x, x) for x in db])
    p = [eye - x for x in d]
    p = [pi + _mm(pi.astype(BF16), x2) for pi, x2 in zip(p, d2b)]
    d4b = _bf([_mm(x, x) for x in d2b])
    t = [pi + _mm(pi.astype(BF16), x4) for pi, x4 in zip(p, d4b)]
    for mask in level_masks:
        eb = _bf([jnp.where(mask, a, 0.0) for a in a_list])
        tb = _bf(t)
        teb = _bf([_mm(ti, ei) for ti, ei in zip(tb, eb)])
        t = [ti - _mm(tei, tbi) for ti, tei, tbi in zip(t, teb, tb)]
    return t


def _gdn_core_kernel(q_ref, k_ref, v_ref, beta_ref, g_ref, z_ref, gain_ref, y_ref, state_ref, *, chunks, sub):
    c = GDN_CHUNK
    dh = GDN_HEAD
    nk = GDN_K_HEADS
    nv = GDN_V_HEADS
    rep = nv // nk

    @pl.when(pl.program_id(1) == 0)
    def _():
        state_ref[...] = jnp.zeros_like(state_ref)

    ri = lax.broadcasted_iota(jnp.int32, (c, c), 0)
    ci = lax.broadcasted_iota(jnp.int32, (c, c), 1)
    causal = ri >= ci
    strict = ri > ci
    eye = (ri == ci).astype(F32)
    tril = causal.astype(F32)
    same = lambda size: (ri // size) == (ci // size)
    base_mask = same(GDN_INV_BLOCK)
    level_masks = []
    size = GDN_INV_BLOCK
    while size < c:
        level_masks.append(same(2 * size) & jnp.logical_not(same(size)))
        size *= 2
    gain = gain_ref[...]

    def group_step(idx, carry):
        rows = [pl.ds(pl.multiple_of((idx * sub + s_) * c, c), c) for s_ in range(sub)]
        heads = range(nv)
        vprobs = [(s_, h) for s_ in range(sub) for h in heads]
        kprobs = [(s_, h) for s_ in range(sub) for h in range(nk)]
        col = lambda x, h: x[:, h:h + 1]

        gcum = [_hdot(tril, g_ref[0, r, :]) for r in rows]
        beta = [beta_ref[0, r, :] for r in rows]
        glast = [x[c - 1:c, :] for x in gcum]
        eg = [jnp.exp(x) for x in gcum]
        kdec = [jnp.exp(gl - x) for gl, x in zip(glast, gcum)]
        eglast = [jnp.exp(gl) for gl in glast]

        q16 = {(s_, h): q_ref[0, rows[s_], pl.ds(h * dh, dh)] for s_, h in kprobs}
        k16 = {(s_, h): k_ref[0, rows[s_], pl.ds(h * dh, dh)] for s_, h in kprobs}
        q = {p: q16[p].astype(F32) for p in kprobs}
        k = {p: k16[p].astype(F32) for p in kprobs}
        qk_kk = {p: _mm_nt(jnp.concatenate([q16[p], k16[p]], axis=0), k16[p]) for p in kprobs}
        qk = {p: qk_kk[p][:c] for p in kprobs}
        kk = {p: qk_kk[p][c:] for p in kprobs}

        decay, a = {}, []
        for s_, h in vprobs:
            gcol = col(gcum[s_], h)
            grow = jnp.sum(gcol * eye, axis=0, keepdims=True)
            decay[s_, h] = jnp.where(causal, jnp.exp(jnp.where(causal, gcol - grow, 0.0)), 0.0)
            a.append(jnp.where(strict, kk[s_, h // rep] * col(beta[s_], h) * decay[s_, h], 0.0))
        t = _unit_lower_inverses(a, eye, base_mask, level_masks)

        rhs = []
        for s_, h in vprobs:
            b_col = col(beta[s_], h)
            v = v_ref[0, rows[s_], pl.ds(h * dh, dh)].astype(F32)
            rhs.append(jnp.concatenate([v * b_col, k[s_, h // rep] * (b_col * col(eg[s_], h))],
                                       axis=-1).astype(BF16))
        sol = dict(zip(vprobs, [_mm(ti, ri_) for ti, ri_ in zip(_bf(t), rhs)]))
        attn_kd = {(s_, h): jnp.concatenate(
            [jnp.where(causal, qk[s_, h // rep] * decay[s_, h], 0.0).astype(BF16),
             (k[s_, h // rep] * col(kdec[s_], h)).astype(BF16).T], axis=0) for s_, h in vprobs}
        wq = {(s_, h): jnp.concatenate([sol[s_, h][:, dh:], q[s_, h // rep] * col(eg[s_], h)],
                                       axis=0).astype(BF16) for s_, h in vprobs}

        state = [state_ref[h] for h in heads]
        for s_ in range(sub):
            ws_qs = [_mm(wq[s_, h], state[h].astype(BF16)) for h in heads]
            v_new = _bf([sol[s_, h][:, :dh] - ws_qs[h][:c] for h in heads])
            av_kv = [_mm(attn_kd[s_, h], v_new[h]) for h in heads]
            o = [ws_qs[h][c:] + av_kv[h][:c] for h in heads]
            state = [state[h] * eglast[s_][:, h:h + 1] + av_kv[h][c:] for h in heads]
            for h in heads:
                on = o[h] * lax.rsqrt(jnp.mean(o[h] * o[h], axis=-1, keepdims=True) + NORM_EPS) * gain
                cols = pl.ds(h * dh, dh)
                y_ref[0, rows[s_], cols] = (on * _silu(z_ref[0, rows[s_], cols].astype(F32))).astype(y_ref.dtype)
        for h in heads:
            state_ref[h] = state[h]
        return carry

    lax.fori_loop(0, chunks // sub, group_step, 0)


def gdn_core(proj, beta, g, norm_gain, casts, *, chunks=8, sub=2):
    b, s, _ = proj.shape
    rows = GDN_CHUNK * chunks
    hv = GDN_V_HEADS
    tile = lambda w, blk=0: pl.BlockSpec((1, rows, w), lambda bi, ni: (bi, ni, blk))
    grid = (b, s // rows)
    cast_in, cast_out, cast_shapes, steps_per_slab = _cast_specs(casts, grid)
    in_specs = [tile(GDN_QK, 0), tile(GDN_QK, 1), tile(GDN_V, 1), tile(hv), tile(hv), tile(GDN_V, 2),
                pl.BlockSpec((1, GDN_HEAD), lambda bi, ni: (0, 0))]
    body = functools.partial(_gdn_core_kernel, chunks=chunks, sub=sub)
    y, *cast = pl.pallas_call(
        _with_casts(body, len(in_specs), 1, len(casts), steps_per_slab), grid=grid,
        in_specs=in_specs + cast_in,
        out_specs=[tile(GDN_V)] + cast_out,
        out_shape=[jax.ShapeDtypeStruct((b, s, GDN_V), BF16)] + cast_shapes,
        scratch_shapes=[pltpu.VMEM((hv, GDN_HEAD, GDN_HEAD), F32)],
        compiler_params=_params("parallel", "arbitrary"),
        name="gdn_core",
    )(proj, proj, proj, beta, g, proj, norm_gain.reshape(1, GDN_HEAD), *[w for w, _, _ in casts])
    return y, cast


def _mix_ffn_kernel(h_ref, y_ref, wmix_ref, g_ref, win_ref, wout_ref, fg_ref, o_ref, *, final_norm):
    h1 = h_ref[...] + _mm(y_ref[...], wmix_ref[...])
    xn = _rmsnorm_rows(h1, g_ref[...]).astype(BF16)
    gate = _mm(xn, win_ref[:, :FFN_HIDDEN])
    up = _mm(xn, win_ref[:, FFN_HIDDEN:])
    act = (_silu(gate) * up).astype(BF16)
    out = h1 + _mm(act, wout_ref[...])
    if final_norm:
        out = _rmsnorm_rows(out, fg_ref[...])
    o_ref[...] = out


def mix_ffn_residual(h, y, w_mix, gain, w_in, w_out, final_gain, *, final_norm, tm=512):
    m, d = h.shape
    kdim = y.shape[1]
    resident = functools.partial(pl.BlockSpec, index_map=lambda i: (0, 0), pipeline_mode=pl.Buffered(1))
    rows = lambda w: pl.BlockSpec((tm, w), lambda i: (i, 0))
    return pl.pallas_call(
        functools.partial(_mix_ffn_kernel, final_norm=final_norm),
        grid=(m // tm,),
        in_specs=[rows(d), rows(kdim), resident((kdim, d)), resident((1, d)),
                  resident((d, 2 * FFN_HIDDEN)), resident((FFN_HIDDEN, d)), resident((1, d))],
        out_specs=rows(d),
        out_shape=jax.ShapeDtypeStruct((m, d), F32),
        compiler_params=_params("parallel"),
        name="mix_ffn_residual",
    )(h, y, w_mix, gain, w_in, w_out, final_gain)


def kernel(x, positions, norm_mix, norm_ffn, norm_final, ret_w_in, ret_gn_gain, ret_w_out,
           gdn_w_in, gdn_conv, gdn_a_log, gdn_dt_bias, gdn_norm_gain, gdn_w_out,
           ffn_w_in, ffn_w_out):
    b, s, d = x.shape
    m = b * s
    h = x.reshape(m, d)
    cos, sin = rope_tables(positions)
    final_gain = norm_final.reshape(1, d)
    proj_w = ret_w_in[0].astype(BF16)
    hv = GDN_V_HEADS
    for i in range(DEPTH):
        j = i // N_MIXERS
        retention = i % N_MIXERS == 0
        gain = norm_mix[i].reshape(1, d)
        casts = [(ret_w_out if retention else gdn_w_out, j, d), (ffn_w_in, i, 2 * FFN_HIDDEN), (ffn_w_out, i, d)]
        if i + 1 < DEPTH:
            casts.append((gdn_w_in, j, GDN_MAIN) if retention else (ret_w_in, j + 1, RET_IN))
        if retention:
            proj = ret_proj(h, gain, proj_w, cos, sin)
            y, cast = retention_core(proj.reshape(b, s, RET_IN), ret_gn_gain[j], casts)
        else:
            proj, beta, g = gdn_proj(h, gain, proj_w, gdn_w_in, j, gdn_conv[j], gdn_a_log[j], gdn_dt_bias[j], seq=s)
            y, cast = gdn_core(proj.reshape(b, s, GDN_MAIN), beta.reshape(b, s, hv), g.reshape(b, s, hv),
                               gdn_norm_gain[j], casts)
        proj_w = cast[3] if i + 1 < DEPTH else None
        w_mix, w_ffn_in, w_ffn_out = cast[:3]
        h = mix_ffn_residual(h, y.reshape(m, -1), w_mix, norm_ffn[i].reshape(1, d), w_ffn_in, w_ffn_out,
                             final_gain, final_norm=(i == DEPTH - 1))
    return h.reshape(b, s, d)
```

```python
import functools

import jax
import jax.numpy as jnp
from jax import lax
from jax.experimental import pallas as pl
from jax.experimental.pallas import tpu as pltpu

F32 = jnp.float32
BF16 = jnp.bfloat16
HIGHEST = lax.Precision.HIGHEST

D_MODEL = 1024
DEPTH = 4
N_MIXERS = 2

RET_HEADS = 4
RET_HEAD_QK = D_MODEL // RET_HEADS
RET_HEAD_V = 2 * RET_HEAD_QK
RET_QK = RET_HEADS * RET_HEAD_QK
RET_V = RET_HEADS * RET_HEAD_V
RET_IN = 2 * RET_QK + 2 * RET_V
RET_CHUNK = 128
ROPE_THETA = 10000.0

GDN_HEAD = 128
GDN_K_HEADS = D_MODEL // GDN_HEAD
GDN_V_HEADS = 2 * GDN_K_HEADS
GDN_QK = GDN_K_HEADS * GDN_HEAD
GDN_V = GDN_V_HEADS * GDN_HEAD
GDN_CONV_DIM = 2 * GDN_QK + GDN_V
GDN_MAIN = GDN_CONV_DIM + GDN_V
GDN_CONV_K = 4
GDN_CHUNK = 64
GDN_INV_BLOCK = 8
PREV_ROWS = 16
GDN_PROJ_GROUP = 256
LANES = 128

FFN_HIDDEN = 2816
NORM_EPS = 1e-6

VMEM_LIMIT = 48 * 1024 * 1024


def _params(*sem):
    return pltpu.CompilerParams(dimension_semantics=sem, vmem_limit_bytes=VMEM_LIMIT)


def _mm(a, b):
    return jnp.dot(a, b, preferred_element_type=F32)


def _mm_nt(a, b):
    return lax.dot_general(a, b, (((1,), (1,)), ((), ())), preferred_element_type=F32)


def _mm_tn(a, b):
    return lax.dot_general(a, b, (((0,), (0,)), ((), ())), preferred_element_type=F32)


def _hdot(a, b):
    return jnp.dot(a, b, precision=HIGHEST, preferred_element_type=F32)


def _bf(xs):
    return [x.astype(BF16) for x in xs]


def _sigmoid(x):
    return 1.0 / (1.0 + jnp.exp(-x))


def _silu(x):
    half = 0.5 * x
    return half + half * jnp.tanh(half)


def _rmsnorm_rows(x, gain):
    return x * lax.rsqrt(jnp.mean(x * x, axis=-1, keepdims=True) + NORM_EPS) * gain


CAST_SLABS = 16


def _with_casts(body, n_in, n_out, n_cast, steps_per_slab):
    def kernel_fn(*refs):
        ins, refs = refs[:n_in], refs[n_in:]
        cast_in, refs = refs[:n_cast], refs[n_cast:]
        outs, refs = refs[:n_out], refs[n_out:]
        cast_out, scratch = refs[:n_cast], refs[n_cast:]
        step = pl.program_id(0) * pl.num_programs(1) + pl.program_id(1)

        @pl.when(step % steps_per_slab == 0)
        def _():
            for src, dst in zip(cast_in, cast_out):
                dst[...] = src[...].astype(dst.dtype)

        body(*ins, *outs, *scratch)
    return kernel_fn


def _cast_specs(casts, grid):
    steps_per_slab = grid[0] * grid[1] // CAST_SLABS
    assert steps_per_slab * CAST_SLABS == grid[0] * grid[1]
    slab_of = lambda i, j: (i * grid[1] + j) // steps_per_slab
    in_specs, out_specs, out_shapes = [], [], []
    for stack, layer, cols in casts:
        rows = stack.shape[1]
        slab = rows // CAST_SLABS
        assert slab * CAST_SLABS == rows and slab % 16 == 0 and cols % LANES == 0
        in_specs.append(pl.BlockSpec((None, slab, cols), lambda i, j, layer=layer: (layer, slab_of(i, j), 0)))
        out_specs.append(pl.BlockSpec((slab, cols), lambda i, j: (slab_of(i, j), 0)))
        out_shapes.append(jax.ShapeDtypeStruct((rows, cols), BF16))
    return in_specs, out_specs, out_shapes, steps_per_slab


def _ret_proj_kernel(x_ref, g_ref, w_ref, cos_ref, sin_ref, o_ref, xn_ref):
    dk = RET_HEAD_QK
    half = dk // 2
    rotary_cols = 2 * RET_QK

    @pl.when(pl.program_id(1) == 0)
    def _():
        xn_ref[...] = _rmsnorm_rows(x_ref[...], g_ref[...]).astype(BF16)
        cos = cos_ref[...]
        sin = sin_ref[...]
        project = lambda head: _mm(xn_ref[...], w_ref[:, pl.ds(head * dk, dk)])
        t_next = project(0)
        for head in range(rotary_cols // dk):
            t = t_next
            t_next = project(head + 1) if (head + 1) * dk < rotary_cols else None
            t1, t2 = t[:, :half], t[:, half:]
            rot = jnp.concatenate([t1 * cos - t2 * sin, t2 * cos + t1 * sin], axis=-1)
            o_ref[:, pl.ds(head * dk, dk)] = rot.astype(o_ref.dtype)
        o_ref[:, rotary_cols:] = _mm(xn_ref[...], w_ref[:, rotary_cols:]).astype(o_ref.dtype)

    @pl.when(pl.program_id(1) != 0)
    def _():
        o_ref[...] = _mm(xn_ref[...], w_ref[...]).astype(o_ref.dtype)


def ret_proj(x, gain, w, cos, sin, *, tm=1024, tn=3072):
    m, d = x.shape
    n = w.shape[1]
    assert tn >= 2 * RET_QK and n % tn == 0
    half = RET_HEAD_QK // 2
    return pl.pallas_call(
        _ret_proj_kernel, grid=(m // tm, n // tn),
        in_specs=[pl.BlockSpec((tm, d), lambda i, j: (i, 0)),
                  pl.BlockSpec((1, d), lambda i, j: (0, 0)),
                  pl.BlockSpec((d, tn), lambda i, j: (0, j)),
                  pl.BlockSpec((tm, half), lambda i, j: (i, 0)),
                  pl.BlockSpec((tm, half), lambda i, j: (i, 0))],
        out_specs=pl.BlockSpec((tm, tn), lambda i, j: (i, j)),
        out_shape=jax.ShapeDtypeStruct((m, n), BF16),
        scratch_shapes=[pltpu.VMEM((tm, d), BF16)],
        compiler_params=_params("parallel", "arbitrary"),
        name="ret_proj",
    )(x, gain, w, cos, sin)


def _rope_table_kernel(pos_ref, freq_ref, cos_ref, sin_ref):
    ang = pos_ref[...].astype(F32) * freq_ref[...]
    cos_ref[...] = jnp.cos(ang)
    sin_ref[...] = jnp.sin(ang)


def rope_tables(positions, *, ts=512):
    m = positions.size
    half = RET_HEAD_QK // 2
    inv_freq = (ROPE_THETA ** (-jnp.arange(half, dtype=F32) / half)).reshape(1, half)
    out = jax.ShapeDtypeStruct((m, half), F32)
    return pl.pallas_call(
        _rope_table_kernel, grid=(m // ts,),
        in_specs=[pl.BlockSpec((ts, 1), lambda i: (i, 0)),
                  pl.BlockSpec((1, half), lambda i: (0, 0))],
        out_specs=[pl.BlockSpec((ts, half), lambda i: (i, 0))] * 2,
        out_shape=[out, out], compiler_params=_params("parallel"),
        name="rope_tables",
    )(positions.reshape(m, 1), inv_freq)


def _retention_kernel(q_ref, k_ref, v_ref, gate_ref, decay_ref, qdec_ref,
                      kdec_ref, cdec_ref, gain_ref, y_ref, state_ref, *, chunks):
    c = RET_CHUNK
    dk, dv = RET_HEAD_QK, RET_HEAD_V
    heads = range(RET_HEADS)

    @pl.when(pl.program_id(1) == 0)
    def _():
        state_ref[...] = jnp.zeros_like(state_ref)

    qd, inner, upd = [], [], []
    for ci in range(chunks):
        rows = pl.ds(ci * c, c)
        for h in heads:
            q = q_ref[0, rows, pl.ds(h * dk, dk)]
            k = k_ref[0, rows, pl.ds(h * dk, dk)]
            v = v_ref[0, rows, pl.ds(h * dv, dv)]
            scores = _mm_nt(q, k) * decay_ref[h]
            inner.append(_mm(scores.astype(BF16), v))
            upd.append(_mm_tn((k.astype(F32) * kdec_ref[h]).astype(BF16), v))
            qd.append((q.astype(F32) * qdec_ref[h]).astype(BF16))

    for ci in range(chunks):
        rows = pl.ds(ci * c, c)
        for h in heads:
            p = ci * RET_HEADS + h
            state = state_ref[h]
            o = inner[p] + _mm(qd[p], state.astype(BF16))
            state_ref[h] = state * cdec_ref[h][0:1, 0:1] + upd[p]
            mu = jnp.mean(o, axis=-1, keepdims=True)
            var = jnp.mean(jnp.square(o - mu), axis=-1, keepdims=True)
            cols = pl.ds(h * dv, dv)
            on = (o - mu) * lax.rsqrt(var + NORM_EPS) * gain_ref[:, cols]
            y_ref[0, rows, cols] = (_silu(gate_ref[0, rows, cols].astype(F32)) * on).astype(y_ref.dtype)


def retention_core(proj, gn_gain, casts, *, chunks=4):
    b, s, _ = proj.shape
    c = RET_CHUNK
    h = RET_HEADS
    rows = c * chunks
    log_gamma = jnp.log1p(-jnp.exp2(-5.0 - jnp.arange(h, dtype=F32)))
    pos = jnp.arange(c, dtype=F32)
    causal = pos[:, None] >= pos[None, :]
    diff = jnp.where(causal, pos[:, None] - pos[None, :], 0.0)
    k_scale = RET_HEAD_QK ** -0.5
    decay = jnp.where(causal, jnp.exp(log_gamma[:, None, None] * diff), 0.0) * k_scale
    qdec = jnp.exp(log_gamma[:, None] * (pos + 1.0))[:, :, None]
    kdec = jnp.exp(log_gamma[:, None] * (c - 1.0 - pos))[:, :, None] * k_scale
    cdec = jnp.broadcast_to(jnp.exp(log_gamma * c)[:, None, None], (h, 8, 128))
    dk, dv = RET_HEAD_QK, RET_HEAD_V
    tile = lambda w, blk: pl.BlockSpec((1, rows, w), lambda bi, ni: (bi, ni, blk))
    whole = lambda *shape: pl.BlockSpec(shape, lambda bi, ni: (0,) * len(shape))
    grid = (b, s // rows)
    cast_in, cast_out, cast_shapes, steps_per_slab = _cast_specs(casts, grid)
    in_specs = [
        tile(RET_QK, 0), tile(RET_QK, 1), tile(RET_V, 1), tile(RET_V, 2),
        whole(h, c, c), whole(h, c, 1), whole(h, c, 1), whole(h, 8, 128), whole(1, RET_V),
    ]
    body = functools.partial(_retention_kernel, chunks=chunks)
    y, *cast = pl.pallas_call(
        _with_casts(body, len(in_specs), 1, len(casts), steps_per_slab), grid=grid,
        in_specs=in_specs + cast_in,
        out_specs=[tile(RET_V, 0)] + cast_out,
        out_shape=[jax.ShapeDtypeStruct((b, s, RET_V), BF16)] + cast_shapes,
        scratch_shapes=[pltpu.VMEM((h, dk, dv), F32)],
        compiler_params=_params("parallel", "arbitrary"),
        name="retention_core",
    )(proj, proj, proj, proj, decay, qdec, kdec, cdec, gn_gain.reshape(1, RET_V), *[w for w, _, _ in casts])
    return y, cast


def _gdn_proj_kernel(x_ref, xp_ref, g_ref, w_ref, ws_ref, cw_ref, alog_ref, dtb_ref,
                     o_ref, beta_ref, gl_ref, xn_ref, ybuf_ref, *, tiles_per_seq):
    i = pl.program_id(0)
    j = pl.program_id(1)
    p = PREV_ROWS
    gw = GDN_PROJ_GROUP
    hv = GDN_V_HEADS

    @pl.when(j == 0)
    def _():
        first = (i % tiles_per_seq) == 0
        xp = _rmsnorm_rows(xp_ref[...], g_ref[...])
        xn_ref[0:p, :] = jnp.where(first, 0.0, xp).astype(BF16)
        xn = _rmsnorm_rows(x_ref[...], g_ref[...]).astype(BF16)
        xn_ref[p:, :] = xn
        ba = _mm(xn, ws_ref[:, :2 * hv].astype(BF16))
        beta_ref[...] = _sigmoid(ba[:, :hv])
        a = ba[:, hv:] + dtb_ref[...]
        softplus = jnp.maximum(a, 0.0) + jnp.log1p(jnp.exp(-jnp.abs(a)))
        gl_ref[...] = -jnp.exp(alog_ref[...]) * softplus

    groups = o_ref.shape[1] // gw
    tm = x_ref.shape[0]

    def project(group):
        return _mm(xn_ref[...], w_ref[:, pl.ds(group * gw, gw)])

    def conv_silu(y, group):
        cw = cw_ref[:, pl.ds(group * gw, gw)]
        ybuf = ybuf_ref.at[group % 2]
        ybuf[...] = y
        acc = y[p:] * cw[GDN_CONV_K - 1:GDN_CONV_K, :]
        for sh in range(1, GDN_CONV_K):
            acc = acc + ybuf[pl.ds(p - sh, tm), :] * cw[GDN_CONV_K - 1 - sh:GDN_CONV_K - sh, :]
        return _silu(acc)

    def l2n(y, scale):
        return y * (lax.rsqrt(jnp.sum(y * y, axis=-1, keepdims=True) + NORM_EPS) * scale)

    def conv_groups(finish):
        y_next = project(0)
        for group in range(groups):
            y = y_next
            if group + 1 < groups:
                y_next = project(group + 1)
            o_ref[:, pl.ds(group * gw, gw)] = finish(conv_silu(y, group), group).astype(o_ref.dtype)

    def qk_finish(s, group):
        scale = GDN_HEAD ** -0.5 if group * gw < GDN_QK else 1.0
        return jnp.concatenate([l2n(s[:, c0:c0 + GDN_HEAD], scale) for c0 in range(0, gw, GDN_HEAD)], axis=-1)

    @pl.when(j == 0)
    def _():
        conv_groups(qk_finish)

    @pl.when(j == 1)
    def _():
        conv_groups(lambda s, group: s)

    @pl.when(j == 2)
    def _():
        o_ref[...] = _mm(xn_ref[p:, :], w_ref[...]).astype(o_ref.dtype)


def gdn_proj(x, gain, w_main, w_in, layer, conv_w, a_log, dt_bias, *, seq, tm=1024):
    m, d = x.shape
    hv = GDN_V_HEADS
    tn = GDN_V
    assert GDN_CONV_DIM == 2 * tn and 2 * GDN_QK == tn and seq % tm == 0
    small = lambda shape: pl.BlockSpec(shape, lambda i, j: (0, 0))
    return pl.pallas_call(
        functools.partial(_gdn_proj_kernel, tiles_per_seq=seq // tm),
        grid=(m // tm, GDN_MAIN // tn),
        in_specs=[
            pl.BlockSpec((tm, d), lambda i, j: (i, 0)),
            pl.BlockSpec((PREV_ROWS, d), lambda i, j: (jnp.maximum(i * (tm // PREV_ROWS) - 1, 0), 0)),
            small((1, d)),
            pl.BlockSpec((d, tn), lambda i, j: (0, j)),
            pl.BlockSpec((None, d, LANES), lambda i, j: (layer, 0, GDN_MAIN // LANES)),
            pl.BlockSpec((GDN_CONV_K, tn), lambda i, j: (0, jnp.minimum(j, 1))),
            small((1, hv)), small((1, hv)),
        ],
        out_specs=[pl.BlockSpec((tm, tn), lambda i, j: (i, j)),
                   pl.BlockSpec((tm, hv), lambda i, j: (i, 0)),
                   pl.BlockSpec((tm, hv), lambda i, j: (i, 0))],
        out_shape=[jax.ShapeDtypeStruct((m, GDN_MAIN), BF16),
                   jax.ShapeDtypeStruct((m, hv), F32), jax.ShapeDtypeStruct((m, hv), F32)],
        scratch_shapes=[pltpu.VMEM((PREV_ROWS + tm, d), BF16),
                        pltpu.VMEM((2, PREV_ROWS + tm, GDN_PROJ_GROUP), F32)],
        compiler_params=_params("parallel", "arbitrary"),
        name="gdn_proj",
    )(x, x, gain, w_main, w_in, conv_w, a_log.reshape(1, hv), dt_bias.reshape(1, hv))


def _unit_lower_inverses(a_list, eye, base_mask, level_masks):
    d = [jnp.where(base_mask, a, 0.0) for a in a_list]
    db = _bf(d)
    d2b = _bf([_mm(x, x) for x in db])
    p = [eye - x for x in d]
    p = [pi + _mm(pi.astype(BF16), x2) for pi, x2 in zip(p, d2b)]
    d4b = _bf([_mm(x, x) for x in d2b])
    t = [pi + _mm(pi.astype(BF16), x4) for pi, x4 in zip(p, d4b)]
    for mask in level_masks:
        eb = _bf([jnp.where(mask, a, 0.0) for a in a_list])
        tb = _bf(t)
        teb = _bf([_mm(ti, ei) for ti, ei in zip(tb, eb)])
        t = [ti - _mm(tei, tbi) for ti, tei, tbi in zip(t, teb, tb)]
    return t


def _gdn_core_kernel(q_ref, k_ref, v_ref, beta_ref, g_ref, z_ref, gain_ref, y_ref, state_ref, *, chunks, sub):
    c = GDN_CHUNK
    dh = GDN_HEAD
    nk = GDN_K_HEADS
    nv = GDN_V_HEADS
    rep = nv // nk

    @pl.when(pl.program_id(1) == 0)
    def _():
        state_ref[...] = jnp.zeros_like(state_ref)

    ri = lax.broadcasted_iota(jnp.int32, (c, c), 0)
    ci = lax.broadcasted_iota(jnp.int32, (c, c), 1)
    causal = ri >= ci
    strict = ri > ci
    eye = (ri == ci).astype(F32)
    tril = causal.astype(F32)
    same = lambda size: (ri // size) == (ci // size)
    base_mask = same(GDN_INV_BLOCK)
    level_masks = []
    size = GDN_INV_BLOCK
    while size < c:
        level_masks.append(same(2 * size) & jnp.logical_not(same(size)))
        size *= 2
    gain = gain_ref[...]

    def group_step(idx, carry):
        rows = [pl.ds(pl.multiple_of((idx * sub + s_) * c, c), c) for s_ in range(sub)]
        heads = range(nv)
        vprobs = [(s_, h) for s_ in range(sub) for h in heads]
        kprobs = [(s_, h) for s_ in range(sub) for h in range(nk)]
        col = lambda x, h: x[:, h:h + 1]

        gcum = [_hdot(tril, g_ref[0, r, :]) for r in rows]
        beta = [beta_ref[0, r, :] for r in rows]
        glast = [x[c - 1:c, :] for x in gcum]
        eg = [jnp.exp(x) for x in gcum]
        kdec = [jnp.exp(gl - x) for gl, x in zip(glast, gcum)]
        eglast = [jnp.exp(gl) for gl in glast]

        q16 = {(s_, h): q_ref[0, rows[s_], pl.ds(h * dh, dh)] for s_, h in kprobs}
        k16 = {(s_, h): k_ref[0, rows[s_], pl.ds(h * dh, dh)] for s_, h in kprobs}
        q = {p: q16[p].astype(F32) for p in kprobs}
        k = {p: k16[p].astype(F32) for p in kprobs}
        qk_kk = {p: _mm_nt(jnp.concatenate([q16[p], k16[p]], axis=0), k16[p]) for p in kprobs}
        qk = {p: qk_kk[p][:c] for p in kprobs}
        kk = {p: qk_kk[p][c:] for p in kprobs}

        decay, a = {}, []
        for s_, h in vprobs:
            gcol = col(gcum[s_], h)
            grow = jnp.sum(gcol * eye, axis=0, keepdims=True)
            decay[s_, h] = jnp.where(causal, jnp.exp(jnp.where(causal, gcol - grow, 0.0)), 0.0)
            a.append(jnp.where(strict, kk[s_, h // rep] * col(beta[s_], h) * decay[s_, h], 0.0))
        t = _unit_lower_inverses(a, eye, base_mask, level_masks)

        rhs = []
        for s_, h in vprobs:
            b_col = col(beta[s_], h)
            v = v_ref[0, rows[s_], pl.ds(h * dh, dh)].astype(F32)
            rhs.append(jnp.concatenate([v * b_col, k[s_, h // rep] * (b_col * col(eg[s_], h))],
                                       axis=-1).astype(BF16))
        sol = dict(zip(vprobs, [_mm(ti, ri_) for ti, ri_ in zip(_bf(t), rhs)]))
        attn_kd = {(s_, h): jnp.concatenate(
            [jnp.where(causal, qk[s_, h // rep] * decay[s_, h], 0.0).astype(BF16),
             (k[s_, h // rep] * col(kdec[s_], h)).astype(BF16).T], axis=0) for s_, h in vprobs}
        wq = {(s_, h): jnp.concatenate([sol[s_, h][:, dh:], q[s_, h // rep] * col(eg[s_], h)],
                                       axis=0).astype(BF16) for s_, h in vprobs}

        state = [state_ref[h] for h in heads]
        for s_ in range(sub):
            ws_qs = [_mm(wq[s_, h], state[h].astype(BF16)) for h in heads]
            v_new = _bf([sol[s_, h][:, :dh] - ws_qs[h][:c] for h in heads])
            av_kv = [_mm(attn_kd[s_, h], v_new[h]) for h in heads]
            o = [ws_qs[h][c:] + av_kv[h][:c] for h in heads]
            state = [state[h] * eglast[s_][:, h:h + 1] + av_kv[h][c:] for h in heads]
            for h in heads:
                on = o[h] * lax.rsqrt(jnp.mean(o[h] * o[h], axis=-1, keepdims=True) + NORM_EPS) * gain
                cols = pl.ds(h * dh, dh)
                y_ref[0, rows[s_], cols] = (on * _silu(z_ref[0, rows[s_], cols].astype(F32))).astype(y_ref.dtype)
        for h in heads:
            state_ref[h] = state[h]
        return carry

    lax.fori_loop(0, chunks // sub, group_step, 0)


def gdn_core(proj, beta, g, norm_gain, casts, *, chunks=8, sub=2):
    b, s, _ = proj.shape
    rows = GDN_CHUNK * chunks
    hv = GDN_V_HEADS
    tile = lambda w, blk=0: pl.BlockSpec((1, rows, w), lambda bi, ni: (bi, ni, blk))
    grid = (b, s // rows)
    cast_in, cast_out, cast_shapes, steps_per_slab = _cast_specs(casts, grid)
    in_specs = [tile(GDN_QK, 0), tile(GDN_QK, 1), tile(GDN_V, 1), tile(hv), tile(hv), tile(GDN_V, 2),
                pl.BlockSpec((1, GDN_HEAD), lambda bi, ni: (0, 0))]
    body = functools.partial(_gdn_core_kernel, chunks=chunks, sub=sub)
    y, *cast = pl.pallas_call(
        _with_casts(body, len(in_specs), 1, len(casts), steps_per_slab), grid=grid,
        in_specs=in_specs + cast_in,
        out_specs=[tile(GDN_V)] + cast_out,
        out_shape=[jax.ShapeDtypeStruct((b, s, GDN_V), BF16)] + cast_shapes,
        scratch_shapes=[pltpu.VMEM((hv, GDN_HEAD, GDN_HEAD), F32)],
        compiler_params=_params("parallel", "arbitrary"),
        name="gdn_core",
    )(proj, proj, proj, beta, g, proj, norm_gain.reshape(1, GDN_HEAD), *[w for w, _, _ in casts])
    return y, cast


def _mix_ffn_kernel(h_ref, y_ref, wmix_ref, g_ref, win_ref, wout_ref, fg_ref, o_ref, *, final_norm):
    h1 = h_ref[...] + _mm(y_ref[...], wmix_ref[...])
    xn = _rmsnorm_rows(h1, g_ref[...]).astype(BF16)
    gate = _mm(xn, win_ref[:, :FFN_HIDDEN])
    up = _mm(xn, win_ref[:, FFN_HIDDEN:])
    act = (_silu(gate) * up).astype(BF16)
    out = h1 + _mm(act, wout_ref[...])
    if final_norm:
        out = _rmsnorm_rows(out, fg_ref[...])
    o_ref[...] = out


def mix_ffn_residual(h, y, w_mix, gain, w_in, w_out, final_gain, *, final_norm, tm=512):
    m, d = h.shape
    kdim = y.shape[1]
    resident = functools.partial(pl.BlockSpec, index_map=lambda i: (0, 0), pipeline_mode=pl.Buffered(1))
    rows = lambda w: pl.BlockSpec((tm, w), lambda i: (i, 0))
    return pl.pallas_call(
        functools.partial(_mix_ffn_kernel, final_norm=final_norm),
        grid=(m // tm,),
        in_specs=[rows(d), rows(kdim), resident((kdim, d)), resident((1, d)),
                  resident((d, 2 * FFN_HIDDEN)), resident((FFN_HIDDEN, d)), resident((1, d))],
        out_specs=rows(d),
        out_shape=jax.ShapeDtypeStruct((m, d), F32),
        compiler_params=_params("parallel"),
        name="mix_ffn_residual",
    )(h, y, w_mix, gain, w_in, w_out, final_gain)


def kernel(x, positions, norm_mix, norm_ffn, norm_final, ret_w_in, ret_gn_gain, ret_w_out,
           gdn_w_in, gdn_conv, gdn_a_log, gdn_dt_bias, gdn_norm_gain, gdn_w_out,
           ffn_w_in, ffn_w_out):
    b, s, d = x.shape
    m = b * s
    h = x.reshape(m, d)
    cos, sin = rope_tables(positions)
    final_gain = norm_final.reshape(1, d)
    ready = {("proj", 0): ret_w_in[0].astype(BF16)}
    hv = GDN_V_HEADS

    def tail_weights(layer):
        mixer_out = ret_w_out if layer % N_MIXERS == 0 else gdn_w_out
        return {("mix", layer): (mixer_out, layer // N_MIXERS, d),
                ("ffn_in", layer): (ffn_w_in, layer, 2 * FFN_HIDDEN), ("ffn_out", layer): (ffn_w_out, layer, d)}

    for i in range(DEPTH):
        j = i // N_MIXERS
        retention = i % N_MIXERS == 0
        gain = norm_mix[i].reshape(1, d)
        wanted = {k: v for k, v in tail_weights(i).items() if k not in ready}
        if i + 1 < DEPTH:
            wanted["proj", i + 1] = (gdn_w_in, j, GDN_MAIN) if retention else (ret_w_in, j + 1, RET_IN)
        if not retention and i + 1 < DEPTH:
            wanted.update(tail_weights(i + 1))
        if retention:
            proj = ret_proj(h, gain, ready["proj", i], cos, sin)
            y, cast = retention_core(proj.reshape(b, s, RET_IN), ret_gn_gain[j], list(wanted.values()))
        else:
            proj, beta, g = gdn_proj(h, gain, ready["proj", i], gdn_w_in, j, gdn_conv[j], gdn_a_log[j],
                                     gdn_dt_bias[j], seq=s)
            y, cast = gdn_core(proj.reshape(b, s, GDN_MAIN), beta.reshape(b, s, hv), g.reshape(b, s, hv),
                               gdn_norm_gain[j], list(wanted.values()))
        ready.update(zip(wanted.keys(), cast))
        h = mix_ffn_residual(h, y.reshape(m, -1), ready["mix", i], norm_ffn[i].reshape(1, d), ready["ffn_in", i],
                             ready["ffn_out", i], final_gain, final_norm=(i == DEPTH - 1))
    return h.reshape(b, s, d)
```
